```python
import jax, jax.numpy as jnp
from jax import lax
import numpy as np

D_MODEL = 2048
BATCH = 8
SEQ = 2048
DEPTH = 2

N_META = 16
CHUNK = 128
PAD = CHUNK - N_META
D_MIX = D_MODEL
RET_HEADS = 8
RET_DK = 128
RET_DV = 128
RET_WIDTH = RET_HEADS * RET_DV
MLSTM_HEADS = 4
MLSTM_DK = 128
MLSTM_DV = 256
MLSTM_WIDTH = MLSTM_HEADS * MLSTM_DV
MLSTM_CONV = 4
FFN_DIM = 5632
FFN_CONV = 3
ROPE_THETA = 10000.0
EPS = 1e-6
IN_SIZES = [RET_HEADS * RET_DK, RET_HEADS * RET_DK, RET_WIDTH, RET_WIDTH,
            MLSTM_HEADS * MLSTM_DK, MLSTM_HEADS * MLSTM_DK, MLSTM_WIDTH, MLSTM_WIDTH,
            MLSTM_HEADS, MLSTM_HEADS]
D_IN = sum(IN_SIZES)
IN_OFFSETS = [int(o) for o in np.cumsum(IN_SIZES)[:-1]]

kernel_name = 'hymba_retention_mlstm_convffn'

F32 = jnp.float32


def rmsnorm(x, g):
    xf = x.astype(F32)
    y = xf * lax.rsqrt(jnp.mean(xf * xf, axis=-1, keepdims=True) + EPS)
    return (y * g.astype(F32)).astype(x.dtype)


def head_norm(x, g):
    mu = jnp.mean(x, axis=-1, keepdims=True)
    xc = x - mu
    y = xc * lax.rsqrt(jnp.mean(xc * xc, axis=-1, keepdims=True) + EPS)
    return y.reshape(x.shape[:2] + (-1,)) * g.astype(F32)


def causal_dwconv(x, w, b):
    K = w.shape[0]
    y = lax.conv_general_dilated(x, w[:, None, :].astype(x.dtype), window_strides=(1,),
                                 padding=[(K - 1, 0)], dimension_numbers=('NWC', 'WIO', 'NWC'),
                                 feature_group_count=x.shape[-1])
    return y + b.astype(x.dtype)


def rotary(x, pos):
    d = x.shape[-1]
    inv = ROPE_THETA ** (-jnp.arange(0, d, 2, dtype=F32) / d)
    ang = pos[:, None] * inv[None, :]
    cos = jnp.cos(ang)[None, :, None, :]
    sin = jnp.sin(ang)[None, :, None, :]
    x1, x2 = x[..., : d // 2], x[..., d // 2:]
    return jnp.concatenate([x1 * cos - x2 * sin, x1 * sin + x2 * cos], axis=-1)


def to_chunks(x, fill=0.0):
    x = jnp.pad(x, [(0, 0), (PAD, 0)] + [(0, 0)] * (x.ndim - 2), constant_values=fill)
    B, Lp = x.shape[:2]
    x = x.reshape((B, Lp // CHUNK, CHUNK) + x.shape[2:])
    return jnp.moveaxis(x, 3, 1)


def from_chunks(y):
    y = jnp.moveaxis(y, 1, 3)
    B, N, C, H, d = y.shape
    return y.reshape(B, N * C, H, d)[:, PAD:]


def retention_chunkwise(q, k, v, log_gamma):
    idx = jnp.arange(CHUNK, dtype=F32)
    rel = idx[:, None] - idx[None, :]
    decay = jnp.where(rel >= 0, jnp.exp(log_gamma[:, None, None] * jnp.maximum(rel, 0.0)), 0.0)
    scores = jnp.einsum('bhncd,bhnsd->bhncs', q, k) * decay[None, :, None]
    intra = jnp.einsum('bhncs,bhnse->bhnce', scores, v)
    zeta = jnp.exp(log_gamma[:, None] * (CHUNK - 1 - idx)[None, :])
    kv = jnp.einsum('bhnsd,bhnse->bhnde', k * zeta[None, :, None, :, None], v)
    chunk_decay = jnp.exp(log_gamma * CHUNK)[None, :, None, None]

    def step(R, kv_n):
        return chunk_decay * R + kv_n, R

    R0 = jnp.zeros(kv.shape[:2] + kv.shape[3:], kv.dtype)
    _, R_prev = lax.scan(step, R0, jnp.moveaxis(kv, 2, 0))
    R_prev = jnp.moveaxis(R_prev, 0, 2)
    xi = jnp.exp(log_gamma[:, None] * (idx + 1.0)[None, :])
    inter = jnp.einsum('bhncd,bhnde->bhnce', q * xi[None, :, None, :, None], R_prev)
    return intra + inter


def mlstm_chunkwise(q, k, v, log_i, log_f):
    b = jnp.cumsum(log_f, axis=-1)
    g = b[..., -1]
    causal = jnp.tril(jnp.ones((CHUNK, CHUNK), dtype=bool))
    D = jnp.where(causal, b[..., :, None] - b[..., None, :] + log_i[..., None, :], -jnp.inf)
    m_intra = jnp.max(D, axis=-1)
    a = g[..., None] - b + log_i
    m_loc = jnp.max(a, axis=-1)
    w = jnp.exp(a - m_loc[..., None])
    kv_loc = jnp.einsum('bhnsd,bhnse->bhnde', k * w[..., None], v)
    n_loc = jnp.einsum('bhns,bhnsd->bhnd', w, k)

    def step(carry, xs):
        C, nv, m = carry
        g_n, m_n, kv_n, nl_n = xs
        m_new = jnp.maximum(g_n + m, m_n)
        s_old = jnp.exp(g_n + m - m_new)
        s_loc = jnp.exp(m_n - m_new)
        C_new = s_old[..., None, None] * C + s_loc[..., None, None] * kv_n
        n_new = s_old[..., None] * nv + s_loc[..., None] * nl_n
        return (C_new, n_new, m_new), (C, nv, m)

    B, H = q.shape[:2]
    init = (jnp.zeros((B, H, q.shape[-1], v.shape[-1]), F32),
            jnp.zeros((B, H, q.shape[-1]), F32),
            jnp.zeros((B, H), F32))
    xs = (jnp.moveaxis(g, 2, 0), jnp.moveaxis(m_loc, 2, 0),
          jnp.moveaxis(kv_loc, 2, 0), jnp.moveaxis(n_loc, 2, 0))
    _, (C_prev, n_prev, m_prev) = lax.scan(step, init, xs)
    C_prev = jnp.moveaxis(C_prev, 0, 2)
    n_prev = jnp.moveaxis(n_prev, 0, 2)
    m_prev = jnp.moveaxis(m_prev, 0, 2)
    m_t = jnp.maximum(b + m_prev[..., None], m_intra)
    s_inter = jnp.exp(b + m_prev[..., None] - m_t)
    P = jnp.einsum('bhncd,bhnsd->bhncs', q, k) * jnp.exp(D - m_t[..., None])
    num = jnp.einsum('bhncs,bhnse->bhnce', P, v) \
        + s_inter[..., None] * jnp.einsum('bhncd,bhnde->bhnce', q, C_prev)
    den = jnp.sum(P, axis=-1) + s_inter * jnp.einsum('bhncd,bhnd->bhnc', q, n_prev)
    return num / jnp.maximum(jnp.abs(den), jnp.exp(-m_t))[..., None]


def hybrid_layer(h, pos, norm_mix, w_in, conv_w, conv_b, b_i, b_f, ret_norm, mlstm_norm,
                 w_out, norm_ffn, w_up, ffn_conv_w, ffn_conv_b, w_down):
    B, L, _ = h.shape
    u = rmsnorm(h, norm_mix)
    proj = u @ w_in.astype(u.dtype)
    q_r, k_r, v_r, g_r, q_m, k_m, v_m, o_m, i_pre, f_pre = jnp.split(proj, IN_OFFSETS, axis=-1)

    log_gamma = jnp.log1p(-jnp.exp2(-5.0 - jnp.arange(RET_HEADS, dtype=F32)))
    qr = rotary(q_r.astype(F32).reshape(B, L, RET_HEADS, RET_DK), pos)
    kr = rotary(k_r.astype(F32).reshape(B, L, RET_HEADS, RET_DK), pos) * (RET_DK ** -0.5)
    vr = v_r.astype(F32).reshape(B, L, RET_HEADS, RET_DV)
    ret = from_chunks(retention_chunkwise(to_chunks(qr), to_chunks(kr), to_chunks(vr), log_gamma))
    ret = head_norm(ret, ret_norm) * jax.nn.silu(g_r.astype(F32))

    qk = jax.nn.silu(causal_dwconv(jnp.concatenate([q_m, k_m], axis=-1), conv_w, conv_b))
    qm, km = jnp.split(qk.astype(F32), [MLSTM_HEADS * MLSTM_DK], axis=-1)
    qm = qm.reshape(B, L, MLSTM_HEADS, MLSTM_DK) * (MLSTM_DK ** -0.5)
    km = km.reshape(B, L, MLSTM_HEADS, MLSTM_DK)
    vm = v_m.astype(F32).reshape(B, L, MLSTM_HEADS, MLSTM_DV)
    log_i = i_pre.astype(F32) + b_i.astype(F32)
    log_f = jax.nn.log_sigmoid(f_pre.astype(F32) + b_f.astype(F32))
    hm = mlstm_chunkwise(to_chunks(qm), to_chunks(km), to_chunks(vm),
                         to_chunks(log_i, -jnp.inf), to_chunks(log_f, 0.0))
    hm = head_norm(from_chunks(hm), mlstm_norm) * jax.nn.sigmoid(o_m.astype(F32))

    mix = jnp.concatenate([ret, hm], axis=-1).astype(h.dtype) @ w_out.astype(h.dtype)
    h = h + mix

    u = rmsnorm(h, norm_ffn)
    up = causal_dwconv(u @ w_up.astype(u.dtype), ffn_conv_w, ffn_conv_b)
    gate, val = jnp.split(up, [FFN_DIM], axis=-1)
    return h + (jax.nn.silu(gate) * val) @ w_down.astype(h.dtype)


def setup_inputs(seed: int = 0) -> dict:
    key = jax.random.key(seed)
    ks = jax.random.split(key, 20)
    nrm = jax.random.normal
    return {
        'x': nrm(ks[0], (BATCH, SEQ, D_MODEL), F32),
        'meta_tokens': nrm(ks[1], (N_META, D_MODEL), F32),
        'norm_mix': 1.0 + 0.02 * nrm(ks[2], (DEPTH, D_MODEL), F32),
        'w_in': nrm(ks[3], (DEPTH, D_MODEL, D_IN), F32) * D_MODEL ** -0.5,
        'mlstm_conv_w': nrm(ks[4], (DEPTH, MLSTM_CONV, 2 * MLSTM_HEADS * MLSTM_DK), F32) * MLSTM_CONV ** -0.5,
        'mlstm_conv_b': 0.02 * nrm(ks[5], (DEPTH, 2 * MLSTM_HEADS * MLSTM_DK), F32),
        'mlstm_b_i': 0.1 * nrm(ks[6], (DEPTH, MLSTM_HEADS), F32),
        'mlstm_b_f': jnp.linspace(3.0, 6.0, MLSTM_HEADS, dtype=F32)[None, :] + 0.1 * nrm(ks[7], (DEPTH, MLSTM_HEADS), F32),
        'ret_norm': 1.0 + 0.02 * nrm(ks[8], (DEPTH, RET_WIDTH), F32),
        'mlstm_norm': 1.0 + 0.02 * nrm(ks[9], (DEPTH, MLSTM_WIDTH), F32),
        'w_out': nrm(ks[10], (DEPTH, D_MIX, D_MODEL), F32) * D_MIX ** -0.5,
        'norm_ffn': 1.0 + 0.02 * nrm(ks[11], (DEPTH, D_MODEL), F32),
        'w_up': nrm(ks[12], (DEPTH, D_MODEL, 2 * FFN_DIM), F32) * D_MODEL ** -0.5,
        'ffn_conv_w': nrm(ks[13], (DEPTH, FFN_CONV, 2 * FFN_DIM), F32) * FFN_CONV ** -0.5,
        'ffn_conv_b': 0.02 * nrm(ks[14], (DEPTH, 2 * FFN_DIM), F32),
        'w_down': nrm(ks[15], (DEPTH, FFN_DIM, D_MODEL), F32) * FFN_DIM ** -0.5,
        'norm_final': 1.0 + 0.02 * nrm(ks[16], (D_MODEL,), F32),
    }


def reference(x, meta_tokens, norm_mix, w_in, mlstm_conv_w, mlstm_conv_b, mlstm_b_i, mlstm_b_f,
              ret_norm, mlstm_norm, w_out, norm_ffn, w_up, ffn_conv_w, ffn_conv_b, w_down, norm_final):
    B = x.shape[0]
    meta = jnp.broadcast_to(meta_tokens.astype(x.dtype)[None], (B, N_META, D_MODEL))
    h = jnp.concatenate([meta, x], axis=1)
    pos = jnp.arange(h.shape[1], dtype=F32)
    for l in range(DEPTH):
        h = hybrid_layer(h, pos, norm_mix[l], w_in[l], mlstm_conv_w[l], mlstm_conv_b[l],
                         mlstm_b_i[l], mlstm_b_f[l], ret_norm[l], mlstm_norm[l], w_out[l],
                         norm_ffn[l], w_up[l], ffn_conv_w[l], ffn_conv_b[l], w_down[l])
    h = rmsnorm(h, norm_final)
    return h[:, N_META:]
```

```python
import functools
import math

import jax
import jax.numpy as jnp
import numpy as np
from jax import lax
from jax.experimental import pallas as pl
from jax.experimental.pallas import tpu as pltpu

F32 = jnp.float32
BF16 = jnp.bfloat16

D_MODEL = 2048
N_META = 16
CHUNK = 128
PAD = CHUNK - N_META
RET_HEADS = 8
RET_DK = 128
RET_DV = 128
RET_WIDTH = RET_HEADS * RET_DV
MLSTM_HEADS = 4
MLSTM_DK = 128
MLSTM_DV = 256
MLSTM_WIDTH = MLSTM_HEADS * MLSTM_DV
MLSTM_QK = 2 * MLSTM_HEADS * MLSTM_DK
MLSTM_CONV = 4
FFN_DIM = 5632
FFN_CONV = 3
ROPE_THETA = 10000.0
EPS = 1e-6
D_MAIN = 4 * RET_WIDTH + MLSTM_QK + 2 * MLSTM_WIDTH
N_GATES = 2 * MLSTM_HEADS

LANES = 128
SUBLANES = 8
BF16_ROWS = 16
VMEM_LIMIT = 56 * 1024 * 1024

TM_PROJ = 1024
TN_PROJ = 1024
TM_OUT = 512
TM_FFN = 512
TF_FFN = 512
RMS_ROWS = 128

RET_LOG_GAMMA = [math.log1p(-(2.0 ** (-5.0 - h))) for h in range(RET_HEADS)]
NEG_INF = float("-inf")


def _params(*sem):
    return pltpu.CompilerParams(dimension_semantics=sem, vmem_limit_bytes=VMEM_LIMIT)


def _rmsnorm_rows(src_ref, gain, dst_ref, dst_off, nrows):
    blk = min(RMS_ROWS, nrows)

    def body(r, carry):
        r0 = pl.multiple_of(r * blk, blk)
        x = src_ref[pl.ds(r0, blk), :]
        ms = jnp.mean(x * x, axis=-1, keepdims=True)
        dst_ref[pl.ds(dst_off + r0, blk), :] = (x * lax.rsqrt(ms + EPS) * gain).astype(BF16)
        return carry

    lax.fori_loop(0, nrows // blk, body, 0)


def _pad_row_mask(chunk_idx, chunks_per_seq):
    row = lax.broadcasted_iota(jnp.int32, (CHUNK, 1), 0)
    return jnp.logical_or(chunk_idx % chunks_per_seq != 0, row >= PAD)


def _inproj_kernel(h_ref, g_ref, w_ref, wg_ref, proj_ref, gates_ref, u_ref):
    @pl.when(pl.program_id(1) == 0)
    def _():
        _rmsnorm_rows(h_ref, g_ref[...], u_ref, 0, TM_PROJ)
        gates_ref[...] = jnp.dot(u_ref[...], wg_ref[...], preferred_element_type=F32)

    proj_ref[...] = jnp.dot(u_ref[...], w_ref[...], preferred_element_type=F32)


def _inproj(h, gain, w_main, w_gate):
    m = h.shape[0]
    return pl.pallas_call(
        _inproj_kernel,
        grid=(m // TM_PROJ, D_MAIN // TN_PROJ),
        in_specs=[
            pl.BlockSpec((TM_PROJ, D_MODEL), lambda i, j: (i, 0)),
            pl.BlockSpec((1, D_MODEL), lambda i, j: (0, 0)),
            pl.BlockSpec((D_MODEL, TN_PROJ), lambda i, j: (0, j)),
            pl.BlockSpec((D_MODEL, LANES), lambda i, j: (0, 0)),
        ],
        out_specs=[
            pl.BlockSpec((TM_PROJ, TN_PROJ), lambda i, j: (i, j)),
            pl.BlockSpec((TM_PROJ, LANES), lambda i, j: (i, 0)),
        ],
        out_shape=[
            jax.ShapeDtypeStruct((m, D_MAIN), F32),
            jax.ShapeDtypeStruct((m, LANES), F32),
        ],
        scratch_shapes=[pltpu.VMEM((TM_PROJ, D_MODEL), BF16)],
        compiler_params=_params("parallel", "arbitrary"),
        name="inproj",
    )(h, gain, w_main, w_gate)


def _head_norm(x, gain):
    mu = jnp.mean(x, axis=-1, keepdims=True)
    xc = x - mu
    return xc * lax.rsqrt(jnp.mean(xc * xc, axis=-1, keepdims=True) + EPS) * gain


def _retention_kernel(q_ref, k_ref, v_ref, g_ref, cos_ref, sin_ref, gn_ref, o_ref, state_ref):
    @pl.when(pl.program_id(1) == 0)
    def _():
        state_ref[...] = jnp.zeros_like(state_ref)

    cos = cos_ref[...]
    sin_signed = sin_ref[...]
    row = lax.broadcasted_iota(jnp.int32, (CHUNK, CHUNK), 0)
    col = lax.broadcasted_iota(jnp.int32, (CHUNK, CHUNK), 1)
    rel = row - col
    rel_f = jnp.maximum(rel, 0).astype(F32)
    row_f = row.astype(F32)

    def rope(x):
        return x * cos + pltpu.roll(x, RET_DK // 2, 1) * sin_signed

    for hd in range(RET_HEADS):
        lg = RET_LOG_GAMMA[hd]
        sl = slice(hd * RET_DK, (hd + 1) * RET_DK)
        q = rope(q_ref[:, sl])
        k = rope(k_ref[:, sl]) * (RET_DK ** -0.5)
        vb = v_ref[:, sl].astype(BF16)
        decay = jnp.where(rel >= 0, jnp.exp(lg * rel_f), 0.0)
        scores = lax.dot_general(q.astype(BF16), k.astype(BF16), (((1,), (1,)), ((), ())),
                                 preferred_element_type=F32) * decay
        intra = jnp.dot(scores.astype(BF16), vb, preferred_element_type=F32)
        zeta = jnp.exp(lg * (CHUNK - 1.0 - row_f))
        kv = lax.dot_general((k * zeta).astype(BF16), vb, (((0,), (0,)), ((), ())),
                             preferred_element_type=F32)
        xi = jnp.exp(lg * (row_f + 1.0))
        state = state_ref[hd]
        inter = jnp.dot((q * xi).astype(BF16), state.astype(BF16), preferred_element_type=F32)
        state_ref[hd] = math.exp(lg * CHUNK) * state + kv
        gate = g_ref[:, sl]
        y = _head_norm(intra + inter, gn_ref[:, sl]) * (gate * jax.nn.sigmoid(gate))
        o_ref[:, sl] = y.astype(BF16)


def _retention(proj, cos, sin_signed, gain, batch, nchunk):
    m = proj.shape[0]
    blk = lambda c: pl.BlockSpec((CHUNK, RET_WIDTH), lambda b, n: (b * nchunk + n, c))
    tab = pl.BlockSpec((CHUNK, RET_DK), lambda b, n: (n, 0))
    return pl.pallas_call(
        _retention_kernel,
        grid=(batch, nchunk),
        in_specs=[blk(0), blk(1), blk(2), blk(3), tab, tab,
                  pl.BlockSpec((1, RET_WIDTH), lambda b, n: (0, 0))],
        out_specs=pl.BlockSpec((CHUNK, RET_WIDTH), lambda b, n: (b * nchunk + n, 0)),
        out_shape=jax.ShapeDtypeStruct((m, RET_WIDTH), BF16),
        scratch_shapes=[pltpu.VMEM((RET_HEADS, RET_DK, RET_DV), F32)],
        compiler_params=_params("parallel", "arbitrary"),
        name="retention",
    )(proj, proj, proj, proj, cos, sin_signed, gain)


def _mlstm_kernel(qk_ref, v_ref, og_ref, gates_ref, gbias_ref, cw_ref, cb_ref, gn_ref, o_ref,
                  xe_ref, c_ref, n_ref, m_ref):
    n = pl.program_id(1)
    hist = SUBLANES

    @pl.when(n == 0)
    def _():
        xe_ref[0:hist, :] = jnp.zeros((hist, MLSTM_QK), F32)
        c_ref[...] = jnp.zeros_like(c_ref)
        n_ref[...] = jnp.zeros_like(n_ref)
        m_ref[...] = jnp.zeros_like(m_ref)

    xe_ref[hist:hist + CHUNK, :] = qk_ref[...]
    cw = cw_ref[...]
    y = cb_ref[...]
    for tap in range(MLSTM_CONV):
        off = hist - (MLSTM_CONV - 1) + tap
        y = y + cw[tap:tap + 1, :] * xe_ref[off:off + CHUNK, :]
    xe_ref[0:hist, :] = xe_ref[CHUNK:CHUNK + hist, :]
    qk = y * jax.nn.sigmoid(y)

    lane = lax.broadcasted_iota(jnp.int32, (CHUNK, LANES), 1)
    rowi = lax.broadcasted_iota(jnp.int32, (CHUNK, LANES), 0)
    valid = jnp.logical_or(n > 0, rowi >= PAD)
    pre = gates_ref[...] + gbias_ref[...]
    log_f = jnp.minimum(pre, 0.0) - jnp.log1p(jnp.exp(-jnp.abs(pre)))
    is_f = jnp.logical_and(lane >= MLSTM_HEADS, lane < N_GATES)
    log_f = jnp.where(jnp.logical_and(is_f, valid), log_f, 0.0)
    log_i = jnp.where(valid, pre, NEG_INF)
    tri = (rowi >= lane).astype(F32)
    b_cols = jnp.dot(tri, log_f, precision=lax.Precision.HIGHEST, preferred_element_type=F32)
    b_rows = b_cols.T
    log_i_rows = log_i.T
    causal = rowi >= lane

    for hd in range(MLSTM_HEADS):
        q = qk[:, hd * MLSTM_DK:(hd + 1) * MLSTM_DK] * (MLSTM_DK ** -0.5)
        k = qk[:, MLSTM_HEADS * MLSTM_DK + hd * MLSTM_DK:MLSTM_HEADS * MLSTM_DK + (hd + 1) * MLSTM_DK]
        vsl = slice(hd * MLSTM_DV, (hd + 1) * MLSTM_DV)
        vb = v_ref[:, vsl].astype(BF16)
        qb = q.astype(BF16)
        fcol = MLSTM_HEADS + hd
        b_col = b_cols[:, fcol:fcol + 1]
        b_row = b_rows[fcol:fcol + 1, :]
        li_col = log_i[:, hd:hd + 1]
        li_row = log_i_rows[hd:hd + 1, :]
        g_tot = b_cols[CHUNK - 1:CHUNK, fcol:fcol + 1]
        m_prev = m_ref[hd][0:1, 0:1]
        c_prev = c_ref[hd]
        n_prev = n_ref[hd][0:1, :]

        dmat = jnp.where(causal, b_col - b_row + li_row, NEG_INF)
        m_intra = jnp.max(dmat, axis=-1, keepdims=True)
        a_col = g_tot - b_col + li_col
        m_loc = jnp.max(a_col, axis=0, keepdims=True)
        kw = k * jnp.exp(a_col - m_loc)
        kv_loc = lax.dot_general(kw.astype(BF16), vb, (((0,), (0,)), ((), ())),
                                 preferred_element_type=F32)
        n_loc = jnp.sum(kw, axis=0, keepdims=True)

        m_t = jnp.maximum(b_col + m_prev, m_intra)
        s_inter = jnp.exp(b_col + m_prev - m_t)
        p = lax.dot_general(qb, k.astype(BF16), (((1,), (1,)), ((), ())),
                            preferred_element_type=F32) * jnp.exp(dmat - m_t)
        num = jnp.dot(p.astype(BF16), vb, preferred_element_type=F32) \
            + s_inter * jnp.dot(qb, c_prev.astype(BF16), preferred_element_type=F32)
        den = jnp.sum(p, axis=-1, keepdims=True) \
            + s_inter * jnp.sum(q * n_prev, axis=-1, keepdims=True)
        hout = num / jnp.maximum(jnp.abs(den), jnp.exp(-m_t))

        m_new = jnp.maximum(g_tot + m_prev, m_loc)
        s_old = jnp.exp(g_tot + m_prev - m_new)
        s_loc = jnp.exp(m_loc - m_new)
        c_ref[hd] = s_old * c_prev + s_loc * kv_loc
        n_ref[hd] = jnp.broadcast_to(s_old * n_prev + s_loc * n_loc, (SUBLANES, MLSTM_DK))
        m_ref[hd] = jnp.broadcast_to(m_new, (SUBLANES, LANES))

        og = og_ref[:, vsl]
        o_ref[:, vsl] = (_head_norm(hout, gn_ref[:, vsl]) * jax.nn.sigmoid(og)).astype(BF16)


def _mlstm(proj, gates, gate_bias, conv_w, conv_b, gain, batch, nchunk):
    m = proj.shape[0]
    qk_blk = 4 * RET_WIDTH // MLSTM_QK
    v_blk = (4 * RET_WIDTH + MLSTM_QK) // MLSTM_WIDTH
    rows = lambda b, n: b * nchunk + n
    const = lambda shape: pl.BlockSpec(shape, lambda b, n: (0, 0))
    return pl.pallas_call(
        _mlstm_kernel,
        grid=(batch, nchunk),
        in_specs=[
            pl.BlockSpec((CHUNK, MLSTM_QK), lambda b, n: (rows(b, n), qk_blk)),
            pl.BlockSpec((CHUNK, MLSTM_WIDTH), lambda b, n: (rows(b, n), v_blk)),
            pl.BlockSpec((CHUNK, MLSTM_WIDTH), lambda b, n: (rows(b, n), v_blk + 1)),
            pl.BlockSpec((CHUNK, LANES), lambda b, n: (rows(b, n), 0)),
            const((1, LANES)),
            const((MLSTM_CONV, MLSTM_QK)),
            const((1, MLSTM_QK)),
            const((1, MLSTM_WIDTH)),
        ],
        out_specs=pl.BlockSpec((CHUNK, MLSTM_WIDTH), lambda b, n: (rows(b, n), 0)),
        out_shape=jax.ShapeDtypeStruct((m, MLSTM_WIDTH), BF16),
        scratch_shapes=[
            pltpu.VMEM((CHUNK + 2 * SUBLANES, MLSTM_QK), F32),
            pltpu.VMEM((MLSTM_HEADS, MLSTM_DK, MLSTM_DV), F32),
            pltpu.VMEM((MLSTM_HEADS, SUBLANES, MLSTM_DK), F32),
            pltpu.VMEM((MLSTM_HEADS, SUBLANES, LANES), F32),
        ],
        compiler_params=_params("parallel", "arbitrary"),
        name="mlstm",
    )(proj, proj, proj, gates, gate_bias, conv_w, conv_b, gain)


def _zero_pad_rows(x, first_chunk, chunks_per_seq):
    parts = []
    for c in range(x.shape[0] // CHUNK):
        keep = _pad_row_mask(first_chunk + c, chunks_per_seq)
        parts.append(jnp.where(keep, x[c * CHUNK:(c + 1) * CHUNK], 0.0))
    return jnp.concatenate(parts, axis=0)


def _outproj_kernel(ret_ref, hm_ref, h_ref, wa_ref, wb_ref, o_ref, *, chunks_per_seq):
    acc = h_ref[...] + jnp.dot(ret_ref[...], wa_ref[...], preferred_element_type=F32)
    acc = acc + jnp.dot(hm_ref[...], wb_ref[...], preferred_element_type=F32)
    first_chunk = pl.program_id(0) * (TM_OUT // CHUNK)
    o_ref[...] = _zero_pad_rows(acc, first_chunk, chunks_per_seq)


def _outproj(ret, hm, h, w_a, w_b, nchunk):
    m = h.shape[0]
    return pl.pallas_call(
        functools.partial(_outproj_kernel, chunks_per_seq=nchunk),
        grid=(m // TM_OUT,),
        in_specs=[
            pl.BlockSpec((TM_OUT, RET_WIDTH), lambda i: (i, 0)),
            pl.BlockSpec((TM_OUT, MLSTM_WIDTH), lambda i: (i, 0)),
            pl.BlockSpec((TM_OUT, D_MODEL), lambda i: (i, 0)),
            pl.BlockSpec((RET_WIDTH, D_MODEL), lambda i: (0, 0)),
            pl.BlockSpec((MLSTM_WIDTH, D_MODEL), lambda i: (0, 0)),
        ],
        out_specs=pl.BlockSpec((TM_OUT, D_MODEL), lambda i: (i, 0)),
        out_shape=jax.ShapeDtypeStruct((m, D_MODEL), F32),
        compiler_params=_params("parallel"),
        name="outproj",
    )(ret, hm, h, w_a, w_b)


def _ffn_kernel(h_ref, halo_ref, g_ref, wg_ref, wv_ref, cwg_ref, cwv_ref, cbg_ref, cbv_ref, wd_ref,
                o_ref, u_ref, upg_ref, upv_ref, *, chunks_per_seq):
    f = pl.program_id(1)
    halo = BF16_ROWS

    @pl.when(f == 0)
    def _():
        gain = g_ref[...]
        _rmsnorm_rows(halo_ref, gain, u_ref, 0, halo)
        _rmsnorm_rows(h_ref, gain, u_ref, halo, TM_FFN)

    u = u_ref[...]
    upg_ref[...] = jnp.dot(u, wg_ref[...], preferred_element_type=F32)
    upv_ref[...] = jnp.dot(u, wv_ref[...], preferred_element_type=F32)

    def conv(up_ref, cw_ref, cb_ref):
        cw = cw_ref[...]
        y = cb_ref[...]
        for tap in range(FFN_CONV):
            off = halo - (FFN_CONV - 1) + tap
            y = y + cw[tap:tap + 1, :] * up_ref[off:off + TM_FFN, :]
        return y

    gate = conv(upg_ref, cwg_ref, cbg_ref)
    val = conv(upv_ref, cwv_ref, cbv_ref)
    act = (gate * jax.nn.sigmoid(gate) * val).astype(BF16)
    part = jnp.dot(act, wd_ref[...], preferred_element_type=F32)

    @pl.when(f == 0)
    def _():
        o_ref[...] = h_ref[...] + part

    @pl.when(f > 0)
    def _():
        o_ref[...] += part

    @pl.when(f == pl.num_programs(1) - 1)
    def _():
        first_chunk = pl.program_id(0) * (TM_FFN // CHUNK)
        o_ref[...] = _zero_pad_rows(o_ref[...], first_chunk, chunks_per_seq)


def _ffn(h, gain, w_up, conv_w, conv_b, w_down, nchunk):
    m = h.shape[0]
    nf = FFN_DIM // TF_FFN
    halo_blocks = TM_FFN // BF16_ROWS
    return pl.pallas_call(
        functools.partial(_ffn_kernel, chunks_per_seq=nchunk),
        grid=(m // TM_FFN, nf),
        in_specs=[
            pl.BlockSpec((TM_FFN, D_MODEL), lambda i, f: (i, 0)),
            pl.BlockSpec((BF16_ROWS, D_MODEL), lambda i, f: (jnp.maximum(i * halo_blocks - 1, 0), 0)),
            pl.BlockSpec((1, D_MODEL), lambda i, f: (0, 0)),
            pl.BlockSpec((D_MODEL, TF_FFN), lambda i, f: (0, f)),
            pl.BlockSpec((D_MODEL, TF_FFN), lambda i, f: (0, nf + f)),
            pl.BlockSpec((FFN_CONV, TF_FFN), lambda i, f: (0, f)),
            pl.BlockSpec((FFN_CONV, TF_FFN), lambda i, f: (0, nf + f)),
            pl.BlockSpec((1, TF_FFN), lambda i, f: (0, f)),
            pl.BlockSpec((1, TF_FFN), lambda i, f: (0, nf + f)),
            pl.BlockSpec((TF_FFN, D_MODEL), lambda i, f: (f, 0)),
        ],
        out_specs=pl.BlockSpec((TM_FFN, D_MODEL), lambda i, f: (i, 0)),
        out_shape=jax.ShapeDtypeStruct((m, D_MODEL), F32),
        scratch_shapes=[
            pltpu.VMEM((TM_FFN + BF16_ROWS, D_MODEL), BF16),
            pltpu.VMEM((TM_FFN + BF16_ROWS, TF_FFN), F32),
            pltpu.VMEM((TM_FFN + BF16_ROWS, TF_FFN), F32),
        ],
        compiler_params=_params("parallel", "arbitrary"),
        name="convffn",
    )(h, h, gain, w_up, w_up, conv_w, conv_w, conv_b, conv_b, w_down)


def _final_norm_kernel(h_ref, g_ref, o_ref):
    x = h_ref[...]
    ms = jnp.mean(x * x, axis=-1, keepdims=True)
    o_ref[...] = x * lax.rsqrt(ms + EPS) * g_ref[...]


def _final_norm(h, gain, batch, nchunk, seq):
    return pl.pallas_call(
        _final_norm_kernel,
        grid=(batch, seq // CHUNK),
        in_specs=[
            pl.BlockSpec((CHUNK, D_MODEL), lambda b, j: (b * nchunk + 1 + j, 0)),
            pl.BlockSpec((1, D_MODEL), lambda b, j: (0, 0)),
        ],
        out_specs=pl.BlockSpec((CHUNK, D_MODEL), lambda b, j: (b * (seq // CHUNK) + j, 0)),
        out_shape=jax.ShapeDtypeStruct((batch * seq, D_MODEL), F32),
        compiler_params=_params("parallel", "parallel"),
        name="final_norm",
    )(h, gain)


def _rope_tables(lp):
    pos = jnp.maximum(jnp.arange(lp, dtype=F32) - PAD, 0.0)
    inv = ROPE_THETA ** (-jnp.arange(0, RET_DK, 2, dtype=F32) / RET_DK)
    ang = pos[:, None] * inv[None, :]
    cos, sin = jnp.cos(ang), jnp.sin(ang)
    return jnp.concatenate([cos, cos], axis=-1), jnp.concatenate([-sin, sin], axis=-1)


def kernel(x, meta_tokens, norm_mix, w_in, mlstm_conv_w, mlstm_conv_b, mlstm_b_i, mlstm_b_f, ret_norm, mlstm_norm, w_out, norm_ffn, w_up, ffn_conv_w, ffn_conv_b, w_down, norm_final):
    batch, seq, d = x.shape
    depth = w_in.shape[0]
    lp = PAD + N_META + seq
    nchunk = lp // CHUNK
    assert d == D_MODEL and seq % CHUNK == 0 and w_in.shape[2] == D_MAIN + N_GATES
    assert (batch * lp) % TM_PROJ == 0 and (batch * lp) % TM_FFN == 0

    meta = jnp.broadcast_to(meta_tokens.astype(x.dtype)[None], (batch, N_META, d))
    h = jnp.concatenate([jnp.zeros((batch, PAD, d), x.dtype), meta, x], axis=1).reshape(batch * lp, d)
    cos, sin_signed = _rope_tables(lp)

    for l in range(depth):
        w_main = w_in[l, :, :D_MAIN].astype(BF16)
        w_gate = jnp.pad(w_in[l, :, D_MAIN:], ((0, 0), (0, LANES - N_GATES))).astype(BF16)
        gate_bias = jnp.pad(jnp.concatenate([mlstm_b_i[l], mlstm_b_f[l]]), (0, LANES - N_GATES))[None, :]
        proj, gates = _inproj(h, norm_mix[l][None, :], w_main, w_gate)
        ret = _retention(proj, cos, sin_signed, ret_norm[l][None, :], batch, nchunk)
        hm = _mlstm(proj, gates, gate_bias, mlstm_conv_w[l], mlstm_conv_b[l][None, :],
                    mlstm_norm[l][None, :], batch, nchunk)
        w_o = w_out[l].astype(BF16)
        h = _outproj(ret, hm, h, w_o[:RET_WIDTH], w_o[RET_WIDTH:], nchunk)
        h = _ffn(h, norm_ffn[l][None, :], w_up[l].astype(BF16), ffn_conv_w[l], ffn_conv_b[l][None, :],
                 w_down[l].astype(BF16), nchunk)

    out = _final_norm(h, norm_final[None, :], batch, nchunk, seq)
    return out.reshape(batch, seq, d)
```

```python
import functools
import math

import jax
import jax.numpy as jnp
import numpy as np
from jax import lax
from jax.experimental import pallas as pl
from jax.experimental.pallas import tpu as pltpu

F32 = jnp.float32
BF16 = jnp.bfloat16

D_MODEL = 2048
N_META = 16
CHUNK = 128
PAD = CHUNK - N_META
RET_HEADS = 8
RET_DK = 128
RET_DV = 128
RET_WIDTH = RET_HEADS * RET_DV
MLSTM_HEADS = 4
MLSTM_DK = 128
MLSTM_DV = 256
MLSTM_WIDTH = MLSTM_HEADS * MLSTM_DV
MLSTM_QK = 2 * MLSTM_HEADS * MLSTM_DK
MLSTM_CONV = 4
FFN_DIM = 5632
FFN_CONV = 3
ROPE_THETA = 10000.0
EPS = 1e-6
D_MAIN = 4 * RET_WIDTH + MLSTM_QK + 2 * MLSTM_WIDTH
N_GATES = 2 * MLSTM_HEADS

LANES = 128
SUBLANES = 8
BF16_ROWS = 16
VMEM_LIMIT = 56 * 1024 * 1024

TM_PROJ = 1024
TN_PROJ = 1024
TM_OUT = 512
TM_FFN = 512
TF_FFN = 512
RMS_ROWS = 128

RET_LOG_GAMMA = [math.log1p(-(2.0 ** (-5.0 - h))) for h in range(RET_HEADS)]
NEG_INF = float("-inf")


def _params(*sem):
    return pltpu.CompilerParams(dimension_semantics=sem, vmem_limit_bytes=VMEM_LIMIT)


def _rmsnorm_rows(src_ref, gain, dst_ref, dst_off, nrows):
    blk = min(RMS_ROWS, nrows)

    def body(r, carry):
        r0 = pl.multiple_of(r * blk, blk)
        x = src_ref[pl.ds(r0, blk), :]
        ms = jnp.mean(x * x, axis=-1, keepdims=True)
        dst_ref[pl.ds(dst_off + r0, blk), :] = (x * lax.rsqrt(ms + EPS) * gain).astype(BF16)
        return carry

    lax.fori_loop(0, nrows // blk, body, 0)


def _pad_row_mask(chunk_idx, chunks_per_seq):
    row = lax.broadcasted_iota(jnp.int32, (CHUNK, 1), 0)
    return jnp.logical_or(chunk_idx % chunks_per_seq != 0, row >= PAD)


def _inproj_kernel(h_ref, g_ref, w_ref, wg_ref, proj_ref, gates_ref, u_ref):
    @pl.when(pl.program_id(1) == 0)
    def _():
        _rmsnorm_rows(h_ref, g_ref[...], u_ref, 0, TM_PROJ)
        gates_ref[...] = jnp.dot(u_ref[...], wg_ref[...], preferred_element_type=F32)

    proj_ref[...] = jnp.dot(u_ref[...], w_ref[...], preferred_element_type=F32)


def _inproj(h, gain, w_in_bf16, layer, w_gate):
    m = h.shape[0]
    return pl.pallas_call(
        _inproj_kernel,
        grid=(m // TM_PROJ, D_MAIN // TN_PROJ),
        in_specs=[
            pl.BlockSpec((TM_PROJ, D_MODEL), lambda i, j: (i, 0)),
            pl.BlockSpec((1, D_MODEL), lambda i, j: (0, 0)),
            pl.BlockSpec((None, D_MODEL, TN_PROJ), lambda i, j: (layer, 0, j)),
            pl.BlockSpec((D_MODEL, LANES), lambda i, j: (0, 0)),
        ],
        out_specs=[
            pl.BlockSpec((TM_PROJ, TN_PROJ), lambda i, j: (i, j)),
            pl.BlockSpec((TM_PROJ, LANES), lambda i, j: (i, 0)),
        ],
        out_shape=[
            jax.ShapeDtypeStruct((m, D_MAIN), F32),
            jax.ShapeDtypeStruct((m, LANES), F32),
        ],
        scratch_shapes=[pltpu.VMEM((TM_PROJ, D_MODEL), BF16)],
        compiler_params=_params("parallel", "arbitrary"),
        name="inproj",
    )(h, gain, w_in_bf16, w_gate)


def _head_norm(x, gain):
    mu = jnp.mean(x, axis=-1, keepdims=True)
    xc = x - mu
    return xc * lax.rsqrt(jnp.mean(xc * xc, axis=-1, keepdims=True) + EPS) * gain


def _retention_kernel(q_ref, k_ref, v_ref, g_ref, cos_ref, sin_ref, cosk_ref, sink_ref, gn_ref, o_ref,
                      state_ref, decay_ref, zeta_ref, xi_ref):
    @pl.when(jnp.logical_and(pl.program_id(0) == 0, pl.program_id(1) == 0))
    def _():
        row = lax.broadcasted_iota(jnp.int32, (CHUNK, CHUNK), 0)
        col = lax.broadcasted_iota(jnp.int32, (CHUNK, CHUNK), 1)
        rel = row - col
        rel_f = jnp.maximum(rel, 0).astype(F32)
        row_f = row.astype(F32)
        for hd in range(RET_HEADS):
            lg = RET_LOG_GAMMA[hd]
            decay_ref[hd] = jnp.where(rel >= 0, jnp.exp(lg * rel_f), 0.0)
            zeta_ref[hd] = jnp.exp(lg * (CHUNK - 1.0 - row_f))
            xi_ref[hd] = jnp.exp(lg * (row_f + 1.0))

    @pl.when(pl.program_id(1) == 0)
    def _():
        state_ref[...] = jnp.zeros_like(state_ref)

    cos = cos_ref[...]
    sin_signed = sin_ref[...]
    cos_k = cosk_ref[...]
    sin_k = sink_ref[...]

    for hd in range(RET_HEADS):
        lg = RET_LOG_GAMMA[hd]
        sl = slice(hd * RET_DK, (hd + 1) * RET_DK)
        q = q_ref[:, sl]
        q = q * cos + pltpu.roll(q, RET_DK // 2, 1) * sin_signed
        k = k_ref[:, sl]
        k = k * cos_k + pltpu.roll(k, RET_DK // 2, 1) * sin_k
        vb = v_ref[:, sl].astype(BF16)
        scores = lax.dot_general(q.astype(BF16), k.astype(BF16), (((1,), (1,)), ((), ())),
                                 preferred_element_type=F32) * decay_ref[hd]
        intra = jnp.dot(scores.astype(BF16), vb, preferred_element_type=F32)
        kv = lax.dot_general((k * zeta_ref[hd]).astype(BF16), vb, (((0,), (0,)), ((), ())),
                             preferred_element_type=F32)
        state = state_ref[hd]
        inter = jnp.dot((q * xi_ref[hd]).astype(BF16), state.astype(BF16), preferred_element_type=F32)
        state_ref[hd] = math.exp(lg * CHUNK) * state + kv
        gate = g_ref[:, sl]
        y = _head_norm(intra + inter, gn_ref[:, sl]) * (gate * jax.nn.sigmoid(gate))
        o_ref[:, sl] = y.astype(BF16)


def _retention(proj, rope, gain, batch, nchunk):
    m = proj.shape[0]
    blk = lambda c: pl.BlockSpec((CHUNK, RET_WIDTH), lambda b, n: (b * nchunk + n, c))
    tab = pl.BlockSpec((CHUNK, RET_DK), lambda b, n: (n, 0))
    return pl.pallas_call(
        _retention_kernel,
        grid=(batch, nchunk),
        in_specs=[blk(0), blk(1), blk(2), blk(3), tab, tab, tab, tab,
                  pl.BlockSpec((1, RET_WIDTH), lambda b, n: (0, 0))],
        out_specs=pl.BlockSpec((CHUNK, RET_WIDTH), lambda b, n: (b * nchunk + n, 0)),
        out_shape=jax.ShapeDtypeStruct((m, RET_WIDTH), BF16),
        scratch_shapes=[pltpu.VMEM((RET_HEADS, RET_DK, RET_DV), F32)]
        + [pltpu.VMEM((RET_HEADS, CHUNK, CHUNK), F32)] * 3,
        compiler_params=_params("arbitrary", "arbitrary"),
        name="retention",
    )(proj, proj, proj, proj, *rope, gain)


def _mlstm_kernel(qk_ref, v_ref, og_ref, gates_ref, gbias_ref, cw_ref, cb_ref, gn_ref, o_ref,
                  xe_ref, c_ref, n_ref, m_ref):
    n = pl.program_id(1)
    hist = SUBLANES

    @pl.when(n == 0)
    def _():
        xe_ref[0:hist, :] = jnp.zeros((hist, MLSTM_QK), F32)
        c_ref[...] = jnp.zeros_like(c_ref)
        n_ref[...] = jnp.zeros_like(n_ref)
        m_ref[...] = jnp.zeros_like(m_ref)

    xe_ref[hist:hist + CHUNK, :] = qk_ref[...]
    cw = cw_ref[...]
    y = cb_ref[...]
    for tap in range(MLSTM_CONV):
        off = hist - (MLSTM_CONV - 1) + tap
        y = y + cw[tap:tap + 1, :] * xe_ref[off:off + CHUNK, :]
    xe_ref[0:hist, :] = xe_ref[CHUNK:CHUNK + hist, :]
    qk = y * jax.nn.sigmoid(y)

    lane = lax.broadcasted_iota(jnp.int32, (CHUNK, LANES), 1)
    rowi = lax.broadcasted_iota(jnp.int32, (CHUNK, LANES), 0)
    valid = jnp.logical_or(n > 0, rowi >= PAD)
    pre = gates_ref[...] + gbias_ref[...]
    log_f = jnp.minimum(pre, 0.0) - jnp.log1p(jnp.exp(-jnp.abs(pre)))
    is_f = jnp.logical_and(lane >= MLSTM_HEADS, lane < N_GATES)
    log_f = jnp.where(jnp.logical_and(is_f, valid), log_f, 0.0)
    log_i = jnp.where(valid, pre, NEG_INF)
    tri = (rowi >= lane).astype(F32)
    b_cols = jnp.dot(tri, log_f, precision=lax.Precision.HIGHEST, preferred_element_type=F32)
    b_rows = b_cols.T
    log_i_rows = log_i.T
    causal = rowi >= lane

    for hd in range(MLSTM_HEADS):
        q = qk[:, hd * MLSTM_DK:(hd + 1) * MLSTM_DK] * (MLSTM_DK ** -0.5)
        k = qk[:, MLSTM_HEADS * MLSTM_DK + hd * MLSTM_DK:MLSTM_HEADS * MLSTM_DK + (hd + 1) * MLSTM_DK]
        vsl = slice(hd * MLSTM_DV, (hd + 1) * MLSTM_DV)
        vb = v_ref[:, vsl].astype(BF16)
        qb = q.astype(BF16)
        fcol = MLSTM_HEADS + hd
        b_col = b_cols[:, fcol:fcol + 1]
        b_row = b_rows[fcol:fcol + 1, :]
        li_col = log_i[:, hd:hd + 1]
        li_row = log_i_rows[hd:hd + 1, :]
        g_tot = b_cols[CHUNK - 1:CHUNK, fcol:fcol + 1]
        m_prev = m_ref[hd][0:1, 0:1]
        c_prev = c_ref[hd]
        n_prev = n_ref[hd][0:1, :]

        dmat = jnp.where(causal, b_col - b_row + li_row, NEG_INF)
        m_intra = jnp.max(dmat, axis=-1, keepdims=True)
        a_col = g_tot - b_col + li_col
        m_loc = jnp.max(a_col, axis=0, keepdims=True)
        kw = k * jnp.exp(a_col - m_loc)
        kv_loc = lax.dot_general(kw.astype(BF16), vb, (((0,), (0,)), ((), ())),
                                 preferred_element_type=F32)
        n_loc = jnp.sum(kw, axis=0, keepdims=True)

        m_t = jnp.maximum(b_col + m_prev, m_intra)
        s_inter = jnp.exp(b_col + m_prev - m_t)
        p = lax.dot_general(qb, k.astype(BF16), (((1,), (1,)), ((), ())),
                            preferred_element_type=F32) * jnp.exp(dmat - m_t)
        num = jnp.dot(p.astype(BF16), vb, preferred_element_type=F32) \
            + s_inter * jnp.dot(qb, c_prev.astype(BF16), preferred_element_type=F32)
        den = jnp.sum(p, axis=-1, keepdims=True) \
            + s_inter * jnp.sum(q * n_prev, axis=-1, keepdims=True)
        hout = num / jnp.maximum(jnp.abs(den), jnp.exp(-m_t))

        m_new = jnp.maximum(g_tot + m_prev, m_loc)
        s_old = jnp.exp(g_tot + m_prev - m_new)
        s_loc = jnp.exp(m_loc - m_new)
        c_ref[hd] = s_old * c_prev + s_loc * kv_loc
        n_ref[hd] = jnp.broadcast_to(s_old * n_prev + s_loc * n_loc, (SUBLANES, MLSTM_DK))
        m_ref[hd] = jnp.broadcast_to(m_new, (SUBLANES, LANES))

        og = og_ref[:, vsl]
        o_ref[:, vsl] = (_head_norm(hout, gn_ref[:, vsl]) * jax.nn.sigmoid(og)).astype(BF16)


def _mlstm(proj, gates, gate_bias, conv_w, conv_b, gain, batch, nchunk):
    m = proj.shape[0]
    qk_blk = 4 * RET_WIDTH // MLSTM_QK
    v_blk = (4 * RET_WIDTH + MLSTM_QK) // MLSTM_WIDTH
    rows = lambda b, n: b * nchunk + n
    const = lambda shape: pl.BlockSpec(shape, lambda b, n: (0, 0))
    return pl.pallas_call(
        _mlstm_kernel,
        grid=(batch, nchunk),
        in_specs=[
            pl.BlockSpec((CHUNK, MLSTM_QK), lambda b, n: (rows(b, n), qk_blk)),
            pl.BlockSpec((CHUNK, MLSTM_WIDTH), lambda b, n: (rows(b, n), v_blk)),
            pl.BlockSpec((CHUNK, MLSTM_WIDTH), lambda b, n: (rows(b, n), v_blk + 1)),
            pl.BlockSpec((CHUNK, LANES), lambda b, n: (rows(b, n), 0)),
            const((1, LANES)),
            const((MLSTM_CONV, MLSTM_QK)),
            const((1, MLSTM_QK)),
            const((1, MLSTM_WIDTH)),
        ],
        out_specs=pl.BlockSpec((CHUNK, MLSTM_WIDTH), lambda b, n: (rows(b, n), 0)),
        out_shape=jax.ShapeDtypeStruct((m, MLSTM_WIDTH), BF16),
        scratch_shapes=[
            pltpu.VMEM((CHUNK + 2 * SUBLANES, MLSTM_QK), F32),
            pltpu.VMEM((MLSTM_HEADS, MLSTM_DK, MLSTM_DV), F32),
            pltpu.VMEM((MLSTM_HEADS, SUBLANES, MLSTM_DK), F32),
            pltpu.VMEM((MLSTM_HEADS, SUBLANES, LANES), F32),
        ],
        compiler_params=_params("parallel", "arbitrary"),
        name="mlstm",
    )(proj, proj, proj, gates, gate_bias, conv_w, conv_b, gain)


def _zero_pad_rows(x, first_chunk, chunks_per_seq):
    parts = []
    for c in range(x.shape[0] // CHUNK):
        keep = _pad_row_mask(first_chunk + c, chunks_per_seq)
        parts.append(jnp.where(keep, x[c * CHUNK:(c + 1) * CHUNK], 0.0))
    return jnp.concatenate(parts, axis=0)


def _outproj_kernel(ret_ref, hm_ref, h_ref, wa_ref, wb_ref, o_ref, *, chunks_per_seq):
    acc = h_ref[...] + jnp.dot(ret_ref[...], wa_ref[...], preferred_element_type=F32)
    acc = acc + jnp.dot(hm_ref[...], wb_ref[...], preferred_element_type=F32)
    first_chunk = pl.program_id(0) * (TM_OUT // CHUNK)
    o_ref[...] = _zero_pad_rows(acc, first_chunk, chunks_per_seq)


def _outproj(ret, hm, h, w_out_bf16, layer, nchunk):
    m = h.shape[0]
    assert RET_WIDTH == MLSTM_WIDTH
    return pl.pallas_call(
        functools.partial(_outproj_kernel, chunks_per_seq=nchunk),
        grid=(m // TM_OUT,),
        in_specs=[
            pl.BlockSpec((TM_OUT, RET_WIDTH), lambda i: (i, 0)),
            pl.BlockSpec((TM_OUT, MLSTM_WIDTH), lambda i: (i, 0)),
            pl.BlockSpec((TM_OUT, D_MODEL), lambda i: (i, 0)),
            pl.BlockSpec((None, RET_WIDTH, D_MODEL), lambda i: (layer, 0, 0)),
            pl.BlockSpec((None, MLSTM_WIDTH, D_MODEL), lambda i: (layer, 1, 0)),
        ],
        out_specs=pl.BlockSpec((TM_OUT, D_MODEL), lambda i: (i, 0)),
        out_shape=jax.ShapeDtypeStruct((m, D_MODEL), F32),
        compiler_params=_params("parallel"),
        name="outproj",
    )(ret, hm, h, w_out_bf16, w_out_bf16)


def _ffn_kernel(h_ref, halo_ref, g_ref, wg_ref, wv_ref, cwg_ref, cwv_ref, cbg_ref, cbv_ref, wd_ref,
                o_ref, u_ref, upg_ref, upv_ref, *, chunks_per_seq):
    f = pl.program_id(1)
    halo = BF16_ROWS

    @pl.when(f == 0)
    def _():
        gain = g_ref[...]
        _rmsnorm_rows(halo_ref, gain, u_ref, 0, halo)
        _rmsnorm_rows(h_ref, gain, u_ref, halo, TM_FFN)
        o_ref[...] = h_ref[...]

    u = u_ref[...]
    upg_ref[...] = jnp.dot(u, wg_ref[...], preferred_element_type=F32)
    upv_ref[...] = jnp.dot(u, wv_ref[...], preferred_element_type=F32)

    def conv(up_ref, cw_ref, cb_ref):
        cw = cw_ref[...]
        y = cb_ref[...]
        for tap in range(FFN_CONV):
            off = halo - (FFN_CONV - 1) + tap
            y = y + cw[tap:tap + 1, :] * up_ref[off:off + TM_FFN, :]
        return y

    gate = conv(upg_ref, cwg_ref, cbg_ref)
    val = conv(upv_ref, cwv_ref, cbv_ref)
    act = (gate * jax.nn.sigmoid(gate) * val).astype(BF16)
    o_ref[...] += jnp.dot(act, wd_ref[...], preferred_element_type=F32)

    @pl.when(f == pl.num_programs(1) - 1)
    def _():
        first_chunk = pl.program_id(0) * (TM_FFN // CHUNK)
        o_ref[...] = _zero_pad_rows(o_ref[...], first_chunk, chunks_per_seq)


def _ffn(h, gain, w_up, conv_w, conv_b, w_down, layer, nchunk):
    m = h.shape[0]
    nf = FFN_DIM // TF_FFN
    halo_blocks = TM_FFN // BF16_ROWS
    return pl.pallas_call(
        functools.partial(_ffn_kernel, chunks_per_seq=nchunk),
        grid=(m // TM_FFN, nf),
        in_specs=[
            pl.BlockSpec((TM_FFN, D_MODEL), lambda i, f: (i, 0)),
            pl.BlockSpec((BF16_ROWS, D_MODEL), lambda i, f: (jnp.maximum(i * halo_blocks - 1, 0), 0)),
            pl.BlockSpec((1, D_MODEL), lambda i, f: (0, 0)),
            pl.BlockSpec((None, D_MODEL, TF_FFN), lambda i, f: (layer, 0, f)),
            pl.BlockSpec((None, D_MODEL, TF_FFN), lambda i, f: (layer, 0, nf + f)),
            pl.BlockSpec((None, FFN_CONV, TF_FFN), lambda i, f: (layer, 0, f)),
            pl.BlockSpec((None, FFN_CONV, TF_FFN), lambda i, f: (layer, 0, nf + f)),
            pl.BlockSpec((None, 1, TF_FFN), lambda i, f: (layer, 0, f)),
            pl.BlockSpec((None, 1, TF_FFN), lambda i, f: (layer, 0, nf + f)),
            pl.BlockSpec((None, TF_FFN, D_MODEL), lambda i, f: (layer, f, 0)),
        ],
        out_specs=pl.BlockSpec((TM_FFN, D_MODEL), lambda i, f: (i, 0)),
        out_shape=jax.ShapeDtypeStruct((m, D_MODEL), F32),
        scratch_shapes=[
            pltpu.VMEM((TM_FFN + BF16_ROWS, D_MODEL), BF16),
            pltpu.VMEM((TM_FFN + BF16_ROWS, TF_FFN), F32),
            pltpu.VMEM((TM_FFN + BF16_ROWS, TF_FFN), F32),
        ],
        compiler_params=_params("parallel", "arbitrary"),
        name="convffn",
    )(h, h, gain, w_up, w_up, conv_w, conv_w, conv_b, conv_b, w_down)


def _final_norm_kernel(h_ref, g_ref, o_ref):
    x = h_ref[...]
    ms = jnp.mean(x * x, axis=-1, keepdims=True)
    o_ref[...] = x * lax.rsqrt(ms + EPS) * g_ref[...]


def _final_norm(h, gain, batch, nchunk, seq):
    return pl.pallas_call(
        _final_norm_kernel,
        grid=(batch, seq // CHUNK),
        in_specs=[
            pl.BlockSpec((CHUNK, D_MODEL), lambda b, j: (b * nchunk + 1 + j, 0)),
            pl.BlockSpec((1, D_MODEL), lambda b, j: (0, 0)),
        ],
        out_specs=pl.BlockSpec((CHUNK, D_MODEL), lambda b, j: (b * (seq // CHUNK) + j, 0)),
        out_shape=jax.ShapeDtypeStruct((batch * seq, D_MODEL), F32),
        compiler_params=_params("parallel", "parallel"),
        name="final_norm",
    )(h, gain)


def _rope_tables(lp):
    pos = jnp.maximum(jnp.arange(lp, dtype=F32) - PAD, 0.0)
    inv = ROPE_THETA ** (-jnp.arange(0, RET_DK, 2, dtype=F32) / RET_DK)
    ang = pos[:, None] * inv[None, :]
    cos, sin = jnp.cos(ang), jnp.sin(ang)
    return jnp.concatenate([cos, cos], axis=-1), jnp.concatenate([-sin, sin], axis=-1)


def kernel(x, meta_tokens, norm_mix, w_in, mlstm_conv_w, mlstm_conv_b, mlstm_b_i, mlstm_b_f, ret_norm, mlstm_norm, w_out, norm_ffn, w_up, ffn_conv_w, ffn_conv_b, w_down, norm_final):
    batch, seq, d = x.shape
    depth = w_in.shape[0]
    lp = PAD + N_META + seq
    nchunk = lp // CHUNK
    assert d == D_MODEL and seq % CHUNK == 0 and w_in.shape[2] == D_MAIN + N_GATES
    assert (batch * lp) % TM_PROJ == 0 and (batch * lp) % TM_FFN == 0

    meta = jnp.broadcast_to(meta_tokens.astype(x.dtype)[None], (batch, N_META, d))
    h = jnp.concatenate([jnp.zeros((batch, PAD, d), x.dtype), meta, x], axis=1).reshape(batch * lp, d)
    cos, sin_signed = _rope_tables(lp)
    rope = (cos, sin_signed, cos * RET_DK ** -0.5, sin_signed * RET_DK ** -0.5)

    w_in_b, w_out_b, w_up_b, w_down_b = (w.astype(BF16) for w in (w_in, w_out, w_up, w_down))
    conv_b3 = ffn_conv_b[:, None, :]

    for l in range(depth):
        w_gate = jnp.pad(w_in[l, :, D_MAIN:], ((0, 0), (0, LANES - N_GATES))).astype(BF16)
        gate_bias = jnp.pad(jnp.concatenate([mlstm_b_i[l], mlstm_b_f[l]]), (0, LANES - N_GATES))[None, :]
        proj, gates = _inproj(h, norm_mix[l][None, :], w_in_b, l, w_gate)
        ret = _retention(proj, rope, ret_norm[l][None, :], batch, nchunk)
        hm = _mlstm(proj, gates, gate_bias, mlstm_conv_w[l], mlstm_conv_b[l][None, :],
                    mlstm_norm[l][None, :], batch, nchunk)
        h = _outproj(ret, hm, h, w_out_b, l, nchunk)
        h = _ffn(h, norm_ffn[l][None, :], w_up_b, ffn_conv_w, conv_b3, w_down_b, l, nchunk)

    out = _final_norm(h, norm_final[None, :], batch, nchunk, seq)
    return out.reshape(batch, seq, d)
```

```python
import functools
import math

import jax
import jax.numpy as jnp
from jax import lax
from jax.experimental import pallas as pl
from jax.experimental.pallas import tpu as pltpu

F32 = jnp.float32
BF16 = jnp.bfloat16

D_MODEL = 2048
N_META = 16
CHUNK = 128
PAD = CHUNK - N_META
RET_HEADS = 8
RET_DK = 128
RET_DV = 128
RET_WIDTH = RET_HEADS * RET_DV
MLSTM_HEADS = 4
MLSTM_DK = 128
MLSTM_DV = 256
MLSTM_WIDTH = MLSTM_HEADS * MLSTM_DV
MLSTM_QK = 2 * MLSTM_HEADS * MLSTM_DK
MLSTM_CONV = 4
FFN_DIM = 5632
FFN_CONV = 3
ROPE_THETA = 10000.0
EPS = 1e-6
D_MAIN = 4 * RET_WIDTH + MLSTM_QK + 2 * MLSTM_WIDTH
N_GATES = 2 * MLSTM_HEADS

LANES = 128
SUBLANES = 8
BF16_ROWS = 16
VMEM_LIMIT = 56 * 1024 * 1024

TM_PROJ = 1024
TN_PROJ = 1024
TM_OUT = 512
TM_FFN = 512
TF_FFN = 512
RMS_ROWS = 128
CONV_HIST = SUBLANES

RET_LOG_GAMMA = [math.log1p(-(2.0 ** (-5.0 - h))) for h in range(RET_HEADS)]
NEG_INF = float("-inf")


def _params(*sem):
    return pltpu.CompilerParams(dimension_semantics=sem, vmem_limit_bytes=VMEM_LIMIT)


def _rmsnorm_rows(src_ref, gain, dst_ref, dst_off, nrows):
    blk = min(RMS_ROWS, nrows)

    def body(r, carry):
        r0 = pl.multiple_of(r * blk, blk)
        x = src_ref[pl.ds(r0, blk), :]
        ms = jnp.mean(x * x, axis=-1, keepdims=True)
        dst_ref[pl.ds(dst_off + r0, blk), :] = (x * lax.rsqrt(ms + EPS) * gain).astype(dst_ref.dtype)
        return carry

    lax.fori_loop(0, nrows // blk, body, 0)


def _zero_pad_rows(x):
    row = lax.broadcasted_iota(jnp.int32, (CHUNK, 1), 0)
    return jnp.where(row >= PAD, x, 0.0)


def _inproj_kernel(h_ref, g_ref, w_ref, wg_ref, proj_ref, gates_ref, u_ref):
    @pl.when(pl.program_id(1) == 0)
    def _():
        _rmsnorm_rows(h_ref, g_ref[...], u_ref, 0, h_ref.shape[0])
        gates_ref[...] = jnp.dot(u_ref[...], wg_ref[...], preferred_element_type=F32)

    proj_ref[...] = jnp.dot(u_ref[...], w_ref[...], preferred_element_type=F32)


def _inproj(h, gain, w_in_bf16, layer, w_gate):
    m = h.shape[0]
    tm = min(TM_PROJ, m)
    return pl.pallas_call(
        _inproj_kernel,
        grid=(m // tm, D_MAIN // TN_PROJ),
        in_specs=[
            pl.BlockSpec((tm, D_MODEL), lambda i, j: (i, 0)),
            pl.BlockSpec((1, D_MODEL), lambda i, j: (0, 0)),
            pl.BlockSpec((None, D_MODEL, TN_PROJ), lambda i, j: (layer, 0, j)),
            pl.BlockSpec((D_MODEL, LANES), lambda i, j: (0, 0)),
        ],
        out_specs=[
            pl.BlockSpec((tm, TN_PROJ), lambda i, j: (i, j)),
            pl.BlockSpec((tm, LANES), lambda i, j: (i, 0)),
        ],
        out_shape=[
            jax.ShapeDtypeStruct((m, D_MAIN), F32),
            jax.ShapeDtypeStruct((m, LANES), F32),
        ],
        scratch_shapes=[pltpu.VMEM((tm, D_MODEL), BF16)],
        compiler_params=_params("parallel", "arbitrary"),
        name="inproj",
    )(h, gain, w_in_bf16, w_gate)


def _head_norm(x, gain):
    mu = jnp.mean(x, axis=-1, keepdims=True)
    xc = x - mu
    return xc * lax.rsqrt(jnp.mean(xc * xc, axis=-1, keepdims=True) + EPS) * gain


def _retention_kernel(q_ref, k_ref, v_ref, g_ref, cos_ref, sin_ref, cosk_ref, sink_ref, gn_ref, init_ref,
                      o_ref, state_ref, decay_ref, zeta_ref, xi_ref):
    @pl.when(jnp.logical_and(pl.program_id(0) == 0, pl.program_id(1) == 0))
    def _():
        row = lax.broadcasted_iota(jnp.int32, (CHUNK, CHUNK), 0)
        col = lax.broadcasted_iota(jnp.int32, (CHUNK, CHUNK), 1)
        rel = row - col
        rel_f = jnp.maximum(rel, 0).astype(F32)
        row_f = row.astype(F32)
        for hd in range(RET_HEADS):
            lg = RET_LOG_GAMMA[hd]
            decay_ref[hd] = jnp.where(rel >= 0, jnp.exp(lg * rel_f), 0.0)
            zeta_ref[hd] = jnp.exp(lg * (CHUNK - 1.0 - row_f))
            xi_ref[hd] = jnp.exp(lg * (row_f + 1.0))

    @pl.when(pl.program_id(1) == 0)
    def _():
        state_ref[...] = init_ref[...]

    cos = cos_ref[...]
    sin_signed = sin_ref[...]
    cos_k = cosk_ref[...]
    sin_k = sink_ref[...]

    heads = range(RET_HEADS)
    sls = [slice(hd * RET_DK, (hd + 1) * RET_DK) for hd in heads]
    scores, kvs, inters, vbs = [], [], [], []
    for hd in heads:
        q = q_ref[:, sls[hd]]
        q = q * cos + pltpu.roll(q, RET_DK // 2, 1) * sin_signed
        k = k_ref[:, sls[hd]]
        k = k * cos_k + pltpu.roll(k, RET_DK // 2, 1) * sin_k
        vb = v_ref[:, sls[hd]].astype(BF16)
        vbs.append(vb)
        scores.append(lax.dot_general(q.astype(BF16), k.astype(BF16), (((1,), (1,)), ((), ())),
                                      preferred_element_type=F32))
        kvs.append(lax.dot_general((k * zeta_ref[hd]).astype(BF16), vb, (((0,), (0,)), ((), ())),
                                   preferred_element_type=F32))
        inters.append(jnp.dot((q * xi_ref[hd]).astype(BF16), state_ref[hd].astype(BF16),
                              preferred_element_type=F32))
    outs = []
    for hd in heads:
        intra = jnp.dot((scores[hd] * decay_ref[hd]).astype(BF16), vbs[hd], preferred_element_type=F32)
        outs.append(intra + inters[hd])
        state_ref[hd] = math.exp(RET_LOG_GAMMA[hd] * CHUNK) * state_ref[hd] + kvs[hd]
    for hd in heads:
        gate = g_ref[:, sls[hd]]
        y = _head_norm(outs[hd], gn_ref[:, sls[hd]]) * (gate * jax.nn.sigmoid(gate))
        o_ref[:, sls[hd]] = y.astype(BF16)


def _retention(proj, rope, gain, init_state, nseq, nchunk):
    m = proj.shape[0]
    blk = lambda c: pl.BlockSpec((CHUNK, RET_WIDTH), lambda b, n: (b * nchunk + n, c))
    tab = pl.BlockSpec((CHUNK, RET_DK), lambda b, n: (n, 0))
    state_spec = pl.BlockSpec((RET_HEADS, RET_DK, RET_DV), lambda b, n: (0, 0, 0))
    return pl.pallas_call(
        _retention_kernel,
        grid=(nseq, nchunk),
        in_specs=[blk(0), blk(1), blk(2), blk(3), tab, tab, tab, tab,
                  pl.BlockSpec((1, RET_WIDTH), lambda b, n: (0, 0)), state_spec],
        out_specs=[pl.BlockSpec((CHUNK, RET_WIDTH), lambda b, n: (b * nchunk + n, 0)), state_spec],
        out_shape=[jax.ShapeDtypeStruct((m, RET_WIDTH), BF16),
                   jax.ShapeDtypeStruct((RET_HEADS, RET_DK, RET_DV), F32)],
        scratch_shapes=[pltpu.VMEM((RET_HEADS, CHUNK, CHUNK), F32)] * 3,
        compiler_params=_params("arbitrary", "arbitrary"),
        name="retention",
    )(proj, proj, proj, proj, *rope, gain, init_state)


def _mlstm_kernel(qk_ref, v_ref, og_ref, gates_ref, gbias_ref, cw_ref, cb_ref, gn_ref,
                  c0_ref, n0_ref, m0_ref, tail0_ref,
                  o_ref, c_ref, n_ref, m_ref, tail_ref, xe_ref, *, mask_pad):
    hist = CONV_HIST

    @pl.when(pl.program_id(1) == 0)
    def _():
        xe_ref[0:hist, :] = tail0_ref[...]
        c_ref[...] = c0_ref[...]
        n_ref[...] = n0_ref[...]
        m_ref[...] = m0_ref[...]

    xe_ref[hist:hist + CHUNK, :] = qk_ref[...]
    cw = cw_ref[...]
    y = cb_ref[...]
    for tap in range(MLSTM_CONV):
        off = hist - (MLSTM_CONV - 1) + tap
        y = y + cw[tap:tap + 1, :] * xe_ref[off:off + CHUNK, :]
    tail = xe_ref[CHUNK:CHUNK + hist, :]
    xe_ref[0:hist, :] = tail
    tail_ref[...] = tail
    qk = y * jax.nn.sigmoid(y)

    lane = lax.broadcasted_iota(jnp.int32, (CHUNK, LANES), 1)
    rowi = lax.broadcasted_iota(jnp.int32, (CHUNK, LANES), 0)
    pre = gates_ref[...] + gbias_ref[...]
    log_f = jnp.minimum(pre, 0.0) - jnp.log1p(jnp.exp(-jnp.abs(pre)))
    is_f = jnp.logical_and(lane >= MLSTM_HEADS, lane < N_GATES)
    log_i = pre
    if mask_pad:
        valid = rowi >= PAD
        is_f = jnp.logical_and(is_f, valid)
        log_i = jnp.where(valid, pre, NEG_INF)
    log_f = jnp.where(is_f, log_f, 0.0)
    tri = (rowi >= lane).astype(F32)
    b_cols = jnp.dot(tri, log_f, precision=lax.Precision.HIGHEST, preferred_element_type=F32)
    b_rows = b_cols.T
    log_i_rows = log_i.T
    causal = rowi >= lane

    for hd in range(MLSTM_HEADS):
        q = qk[:, hd * MLSTM_DK:(hd + 1) * MLSTM_DK] * (MLSTM_DK ** -0.5)
        k = qk[:, MLSTM_HEADS * MLSTM_DK + hd * MLSTM_DK:MLSTM_HEADS * MLSTM_DK + (hd + 1) * MLSTM_DK]
        vsl = slice(hd * MLSTM_DV, (hd + 1) * MLSTM_DV)
        vb = v_ref[:, vsl].astype(BF16)
        qb = q.astype(BF16)
        fcol = MLSTM_HEADS + hd
        b_col = b_cols[:, fcol:fcol + 1]
        b_row = b_rows[fcol:fcol + 1, :]
        li_col = log_i[:, hd:hd + 1]
        li_row = log_i_rows[hd:hd + 1, :]
        g_tot = b_cols[CHUNK - 1:CHUNK, fcol:fcol + 1]
        m_prev = m_ref[hd][0:1, 0:1]
        c_prev = c_ref[hd]
        n_prev = n_ref[hd][0:1, :]

        dmat = jnp.where(causal, b_col - b_row + li_row, NEG_INF)
        m_intra = jnp.max(dmat, axis=-1, keepdims=True)
        a_col = g_tot - b_col + li_col
        m_loc = jnp.max(a_col, axis=0, keepdims=True)
        kw = k * jnp.exp(a_col - m_loc)
        kv_loc = lax.dot_general(kw.astype(BF16), vb, (((0,), (0,)), ((), ())),
                                 preferred_element_type=F32)
        n_loc = jnp.sum(kw, axis=0, keepdims=True)

        m_t = jnp.maximum(b_col + m_prev, m_intra)
        s_inter = jnp.exp(b_col + m_prev - m_t)
        p = lax.dot_general(qb, k.astype(BF16), (((1,), (1,)), ((), ())),
                            preferred_element_type=F32) * jnp.exp(dmat - m_t)
        num = jnp.dot(p.astype(BF16), vb, preferred_element_type=F32) \
            + s_inter * jnp.dot(qb, c_prev.astype(BF16), preferred_element_type=F32)
        den = jnp.sum(p, axis=-1, keepdims=True) \
            + s_inter * jnp.sum(q * n_prev, axis=-1, keepdims=True)
        hout = num / jnp.maximum(jnp.abs(den), jnp.exp(-m_t))

        m_new = jnp.maximum(g_tot + m_prev, m_loc)
        s_old = jnp.exp(g_tot + m_prev - m_new)
        s_loc = jnp.exp(m_loc - m_new)
        c_ref[hd] = s_old * c_prev + s_loc * kv_loc
        n_ref[hd] = jnp.broadcast_to(s_old * n_prev + s_loc * n_loc, (SUBLANES, MLSTM_DK))
        m_ref[hd] = jnp.broadcast_to(m_new, (SUBLANES, LANES))

        og = og_ref[:, vsl]
        o_ref[:, vsl] = (_head_norm(hout, gn_ref[:, vsl]) * jax.nn.sigmoid(og)).astype(BF16)


def _mlstm_state_shapes():
    return [jax.ShapeDtypeStruct((MLSTM_HEADS, MLSTM_DK, MLSTM_DV), F32),
            jax.ShapeDtypeStruct((MLSTM_HEADS, SUBLANES, MLSTM_DK), F32),
            jax.ShapeDtypeStruct((MLSTM_HEADS, SUBLANES, LANES), F32),
            jax.ShapeDtypeStruct((CONV_HIST, MLSTM_QK), F32)]


def _mlstm(proj, gates, gate_bias, conv_w, conv_b, gain, init, nseq, nchunk, mask_pad):
    m = proj.shape[0]
    qk_blk = 4 * RET_WIDTH // MLSTM_QK
    v_blk = (4 * RET_WIDTH + MLSTM_QK) // MLSTM_WIDTH
    rows = lambda b, n: b * nchunk + n
    const = lambda shape: pl.BlockSpec(shape, lambda b, n: (0,) * len(shape))
    state_shapes = _mlstm_state_shapes()
    state_specs = [const(s.shape) for s in state_shapes]
    return pl.pallas_call(
        functools.partial(_mlstm_kernel, mask_pad=mask_pad),
        grid=(nseq, nchunk),
        in_specs=[
            pl.BlockSpec((CHUNK, MLSTM_QK), lambda b, n: (rows(b, n), qk_blk)),
            pl.BlockSpec((CHUNK, MLSTM_WIDTH), lambda b, n: (rows(b, n), v_blk)),
            pl.BlockSpec((CHUNK, MLSTM_WIDTH), lambda b, n: (rows(b, n), v_blk + 1)),
            pl.BlockSpec((CHUNK, LANES), lambda b, n: (rows(b, n), 0)),
            const((1, LANES)),
            const((MLSTM_CONV, MLSTM_QK)),
            const((1, MLSTM_QK)),
            const((1, MLSTM_WIDTH)),
        ] + state_specs,
        out_specs=[pl.BlockSpec((CHUNK, MLSTM_WIDTH), lambda b, n: (rows(b, n), 0))] + state_specs,
        out_shape=[jax.ShapeDtypeStruct((m, MLSTM_WIDTH), BF16)] + state_shapes,
        scratch_shapes=[pltpu.VMEM((CHUNK + 2 * SUBLANES, MLSTM_QK), F32)],
        compiler_params=_params("arbitrary", "arbitrary"),
        name="mlstm",
    )(proj, proj, proj, gates, gate_bias, conv_w, conv_b, gain, *init)


def _outproj_kernel(ret_ref, hm_ref, h_ref, wa_ref, wb_ref, o_ref, *, is_meta):
    acc = h_ref[...] + jnp.dot(ret_ref[...], wa_ref[...], preferred_element_type=F32)
    acc = acc + jnp.dot(hm_ref[...], wb_ref[...], preferred_element_type=F32)
    o_ref[...] = _zero_pad_rows(acc) if is_meta else acc


def _outproj(ret, hm, h, w_out_bf16, layer, is_meta):
    m = h.shape[0]
    tm = min(TM_OUT, m)
    assert RET_WIDTH == MLSTM_WIDTH
    assert not is_meta or m == CHUNK
    return pl.pallas_call(
        functools.partial(_outproj_kernel, is_meta=is_meta),
        grid=(m // tm,),
        in_specs=[
            pl.BlockSpec((tm, RET_WIDTH), lambda i: (i, 0)),
            pl.BlockSpec((tm, MLSTM_WIDTH), lambda i: (i, 0)),
            pl.BlockSpec((tm, D_MODEL), lambda i: (i, 0)),
            pl.BlockSpec((None, RET_WIDTH, D_MODEL), lambda i: (layer, 0, 0)),
            pl.BlockSpec((None, MLSTM_WIDTH, D_MODEL), lambda i: (layer, 1, 0)),
        ],
        out_specs=pl.BlockSpec((tm, D_MODEL), lambda i: (i, 0)),
        out_shape=jax.ShapeDtypeStruct((m, D_MODEL), F32),
        compiler_params=_params("parallel"),
        name="outproj",
    )(ret, hm, h, w_out_bf16, w_out_bf16)


def _ffn_kernel(h_ref, halo_ref, meta_halo_ref, g_ref, wg_ref, wv_ref, cwg_ref, cwv_ref, cbg_ref, cbv_ref,
                wd_ref, fg_ref, o_ref, u_ref, upg_ref, upv_ref, *, tiles_per_seq, is_meta, final_norm):
    f = pl.program_id(1)
    halo = BF16_ROWS
    tm = h_ref.shape[0]

    @pl.when(f == 0)
    def _():
        gain = g_ref[...]
        starts_seq = jnp.logical_and(not is_meta, pl.program_id(0) % tiles_per_seq == 0)

        @pl.when(starts_seq)
        def _():
            _rmsnorm_rows(meta_halo_ref, gain, u_ref, 0, halo)

        @pl.when(jnp.logical_not(starts_seq))
        def _():
            _rmsnorm_rows(halo_ref, gain, u_ref, 0, halo)

        _rmsnorm_rows(h_ref, gain, u_ref, halo, tm)
        o_ref[...] = h_ref[...]

    u = u_ref[...]
    upg_ref[...] = jnp.dot(u, wg_ref[...], preferred_element_type=F32)
    upv_ref[...] = jnp.dot(u, wv_ref[...], preferred_element_type=F32)

    def conv(up_ref, cw_ref, cb_ref):
        cw = cw_ref[...]
        y = cb_ref[...]
        for tap in range(FFN_CONV):
            off = halo - (FFN_CONV - 1) + tap
            y = y + cw[tap:tap + 1, :] * up_ref[off:off + tm, :]
        return y

    gate = conv(upg_ref, cwg_ref, cbg_ref)
    val = conv(upv_ref, cwv_ref, cbv_ref)
    act = (gate * jax.nn.sigmoid(gate) * val).astype(BF16)
    o_ref[...] += jnp.dot(act, wd_ref[...], preferred_element_type=F32)

    if is_meta or final_norm:
        @pl.when(f == pl.num_programs(1) - 1)
        def _():
            if is_meta:
                o_ref[...] = _zero_pad_rows(o_ref[...])
            if final_norm:
                _rmsnorm_rows(o_ref, fg_ref[...], o_ref, 0, tm)


def _ffn(h, meta_h, gain, w_up, conv_w, conv_b, w_down, layer, final_gain, seq, is_meta, final_norm):
    m = h.shape[0]
    tm = min(TM_FFN, m)
    nf = FFN_DIM // TF_FFN
    halo_blocks = tm // BF16_ROWS
    assert not is_meta or m == CHUNK
    return pl.pallas_call(
        functools.partial(_ffn_kernel, tiles_per_seq=seq // tm, is_meta=is_meta, final_norm=final_norm),
        grid=(m // tm, nf),
        in_specs=[
            pl.BlockSpec((tm, D_MODEL), lambda i, f: (i, 0)),
            pl.BlockSpec((BF16_ROWS, D_MODEL), lambda i, f: (jnp.maximum(i * halo_blocks - 1, 0), 0)),
            pl.BlockSpec((BF16_ROWS, D_MODEL), lambda i, f: (CHUNK // BF16_ROWS - 1, 0)),
            pl.BlockSpec((1, D_MODEL), lambda i, f: (0, 0)),
            pl.BlockSpec((None, D_MODEL, TF_FFN), lambda i, f: (layer, 0, f)),
            pl.BlockSpec((None, D_MODEL, TF_FFN), lambda i, f: (layer, 0, nf + f)),
            pl.BlockSpec((None, FFN_CONV, TF_FFN), lambda i, f: (layer, 0, f)),
            pl.BlockSpec((None, FFN_CONV, TF_FFN), lambda i, f: (layer, 0, nf + f)),
            pl.BlockSpec((None, 1, TF_FFN), lambda i, f: (layer, 0, f)),
            pl.BlockSpec((None, 1, TF_FFN), lambda i, f: (layer, 0, nf + f)),
            pl.BlockSpec((None, TF_FFN, D_MODEL), lambda i, f: (layer, f, 0)),
            pl.BlockSpec((1, D_MODEL), lambda i, f: (0, 0)),
        ],
        out_specs=pl.BlockSpec((tm, D_MODEL), lambda i, f: (i, 0)),
        out_shape=jax.ShapeDtypeStruct((m, D_MODEL), F32),
        scratch_shapes=[
            pltpu.VMEM((tm + BF16_ROWS, D_MODEL), BF16),
            pltpu.VMEM((tm + BF16_ROWS, TF_FFN), F32),
            pltpu.VMEM((tm + BF16_ROWS, TF_FFN), F32),
        ],
        compiler_params=_params("parallel", "arbitrary"),
        name="convffn",
    )(h, h, meta_h, gain, w_up, w_up, conv_w, conv_w, conv_b, conv_b, w_down, final_gain)


def _rope_tables(pos):
    inv = ROPE_THETA ** (-jnp.arange(0, RET_DK, 2, dtype=F32) / RET_DK)
    ang = pos[:, None] * inv[None, :]
    cos = jnp.concatenate([jnp.cos(ang)] * 2, axis=-1)
    sin_signed = jnp.concatenate([-jnp.sin(ang), jnp.sin(ang)], axis=-1)
    return cos, sin_signed, cos * RET_DK ** -0.5, sin_signed * RET_DK ** -0.5


def kernel(x, meta_tokens, norm_mix, w_in, mlstm_conv_w, mlstm_conv_b, mlstm_b_i, mlstm_b_f, ret_norm, mlstm_norm, w_out, norm_ffn, w_up, ffn_conv_w, ffn_conv_b, w_down, norm_final):
    batch, seq, d = x.shape
    depth = w_in.shape[0]
    assert d == D_MODEL and w_in.shape[2] == D_MAIN + N_GATES and meta_tokens.shape[0] == N_META
    assert seq % TM_FFN == 0 and (batch * seq) % TM_PROJ == 0

    h = x.reshape(batch * seq, d)
    h_meta = jnp.concatenate([jnp.zeros((PAD, d), x.dtype), meta_tokens.astype(x.dtype)], axis=0)
    rope_meta = _rope_tables(jnp.maximum(jnp.arange(CHUNK, dtype=F32) - PAD, 0.0))
    rope_seq = _rope_tables(jnp.arange(N_META, N_META + seq, dtype=F32))
    ret_zero = jnp.zeros((RET_HEADS, RET_DK, RET_DV), F32)
    mlstm_zero = [jnp.zeros(s.shape, s.dtype) for s in _mlstm_state_shapes()]

    w_in_b, w_out_b, w_up_b, w_down_b = (w.astype(BF16) for w in (w_in, w_out, w_up, w_down))
    conv_b3 = ffn_conv_b[:, None, :]
    final_gain = norm_final[None, :]

    for l in range(depth):
        last = l == depth - 1
        w_gate = jnp.pad(w_in[l, :, D_MAIN:], ((0, 0), (0, LANES - N_GATES))).astype(BF16)
        gate_bias = jnp.pad(jnp.concatenate([mlstm_b_i[l], mlstm_b_f[l]]), (0, LANES - N_GATES))[None, :]
        mix_gain, ret_gain, mlstm_gain = norm_mix[l][None, :], ret_norm[l][None, :], mlstm_norm[l][None, :]
        conv_w, conv_b = mlstm_conv_w[l], mlstm_conv_b[l][None, :]
        ffn_args = (norm_ffn[l][None, :], w_up_b, ffn_conv_w, conv_b3, w_down_b, l, final_gain, seq)

        proj, gates = _inproj(h_meta, mix_gain, w_in_b, l, w_gate)
        ret, ret_state = _retention(proj, rope_meta, ret_gain, ret_zero, 1, 1)
        hm, *mlstm_state = _mlstm(proj, gates, gate_bias, conv_w, conv_b, mlstm_gain, mlstm_zero, 1, 1, True)
        h_meta = _outproj(ret, hm, h_meta, w_out_b, l, True)
        meta_ffn_in = h_meta
        if not last:
            h_meta = _ffn(h_meta, meta_ffn_in, *ffn_args, True, False)

        proj, gates = _inproj(h, mix_gain, w_in_b, l, w_gate)
        ret, _ = _retention(proj, rope_seq, ret_gain, ret_state, batch, seq // CHUNK)
        hm, *_ = _mlstm(proj, gates, gate_bias, conv_w, conv_b, mlstm_gain, mlstm_state,
                        batch, seq // CHUNK, False)
        h = _outproj(ret, hm, h, w_out_b, l, False)
        h = _ffn(h, meta_ffn_in, *ffn_args, False, last)

    return h.reshape(batch, seq, d)
```

```python
import functools
import math

import jax
import jax.numpy as jnp
from jax import lax
from jax.experimental import pallas as pl
from jax.experimental.pallas import tpu as pltpu

F32 = jnp.float32
BF16 = jnp.bfloat16

D_MODEL = 2048
N_META = 16
CHUNK = 128
PAD = CHUNK - N_META
RET_HEADS = 8
RET_DK = 128
RET_DV = 128
RET_WIDTH = RET_HEADS * RET_DV
MLSTM_HEADS = 4
MLSTM_DK = 128
MLSTM_DV = 256
MLSTM_WIDTH = MLSTM_HEADS * MLSTM_DV
D_MIX = RET_WIDTH + MLSTM_WIDTH
MLSTM_QK = 2 * MLSTM_HEADS * MLSTM_DK
MLSTM_CONV = 4
FFN_DIM = 5632
FFN_CONV = 3
ROPE_THETA = 10000.0
EPS = 1e-6
D_MAIN = 4 * RET_WIDTH + MLSTM_QK + 2 * MLSTM_WIDTH
N_GATES = 2 * MLSTM_HEADS

LANES = 128
SUBLANES = 8
BF16_ROWS = 16
VMEM_LIMIT = 56 * 1024 * 1024

TM_PROJ = 1024
TN_PROJ = 1792
TM_OUT = 512
TM_FFN = 512
TF_FFN = 512
RMS_ROWS = 128
CONV_HIST = SUBLANES

RET_LOG_GAMMA = [math.log1p(-(2.0 ** (-5.0 - h))) for h in range(RET_HEADS)]
NEG_INF = float("-inf")


def _params(*sem):
    return pltpu.CompilerParams(dimension_semantics=sem, vmem_limit_bytes=VMEM_LIMIT)


def _rmsnorm_rows(src_ref, gain, dst_ref, dst_off, nrows):
    blk = min(RMS_ROWS, nrows)

    def body(r, carry):
        r0 = pl.multiple_of(r * blk, blk)
        x = src_ref[pl.ds(r0, blk), :]
        ms = jnp.mean(x * x, axis=-1, keepdims=True)
        dst_ref[pl.ds(dst_off + r0, blk), :] = (x * lax.rsqrt(ms + EPS) * gain).astype(dst_ref.dtype)
        return carry

    lax.fori_loop(0, nrows // blk, body, 0)


def _zero_pad_rows(x):
    row = lax.broadcasted_iota(jnp.int32, (CHUNK, 1), 0)
    return jnp.where(row >= PAD, x, 0.0)


def _inproj_kernel(h_ref, g_ref, w_ref, wg_ref, proj_ref, gates_ref, u_ref):
    @pl.when(pl.program_id(1) == 0)
    def _():
        _rmsnorm_rows(h_ref, g_ref[...], u_ref, 0, h_ref.shape[0])
        gates_ref[...] = jnp.dot(u_ref[...], wg_ref[...], preferred_element_type=F32)

    proj_ref[...] = jnp.dot(u_ref[...], w_ref[...], preferred_element_type=F32)


def _inproj(h, gain, w_in_bf16, layer, w_gate):
    m = h.shape[0]
    tm = min(TM_PROJ, m)
    return pl.pallas_call(
        _inproj_kernel,
        grid=(m // tm, D_MAIN // TN_PROJ),
        in_specs=[
            pl.BlockSpec((tm, D_MODEL), lambda i, j: (i, 0)),
            pl.BlockSpec((1, D_MODEL), lambda i, j: (0, 0)),
            pl.BlockSpec((None, D_MODEL, TN_PROJ), lambda i, j: (layer, 0, j)),
            pl.BlockSpec((D_MODEL, LANES), lambda i, j: (0, 0)),
        ],
        out_specs=[
            pl.BlockSpec((tm, TN_PROJ), lambda i, j: (i, j)),
            pl.BlockSpec((tm, LANES), lambda i, j: (i, 0)),
        ],
        out_shape=[
            jax.ShapeDtypeStruct((m, D_MAIN), F32),
            jax.ShapeDtypeStruct((m, LANES), F32),
        ],
        scratch_shapes=[pltpu.VMEM((tm, D_MODEL), BF16)],
        compiler_params=_params("parallel", "arbitrary"),
        name="inproj",
    )(h, gain, w_in_bf16, w_gate)


def _head_norm(x, gain):
    mu = jnp.mean(x, axis=-1, keepdims=True)
    xc = x - mu
    return xc * lax.rsqrt(jnp.mean(xc * xc, axis=-1, keepdims=True) + EPS) * gain


def _retention_kernel(q_ref, k_ref, v_ref, g_ref, cos_ref, sin_ref, cosk_ref, sink_ref, gn_ref, init_ref,
                      o_ref, state_ref, decay_ref, zeta_ref, xi_ref):
    @pl.when(jnp.logical_and(pl.program_id(0) == 0, pl.program_id(1) == 0))
    def _():
        row = lax.broadcasted_iota(jnp.int32, (CHUNK, CHUNK), 0)
        col = lax.broadcasted_iota(jnp.int32, (CHUNK, CHUNK), 1)
        rel = row - col
        rel_f = jnp.maximum(rel, 0).astype(F32)
        row_f = row.astype(F32)
        for hd in range(RET_HEADS):
            lg = RET_LOG_GAMMA[hd]
            decay_ref[hd] = jnp.where(rel >= 0, jnp.exp(lg * rel_f), 0.0)
            zeta_ref[hd] = jnp.exp(lg * (CHUNK - 1.0 - row_f))
            xi_ref[hd] = jnp.exp(lg * (row_f + 1.0))

    @pl.when(pl.program_id(1) == 0)
    def _():
        state_ref[...] = init_ref[...]

    cos = cos_ref[...]
    sin_signed = sin_ref[...]
    cos_k = cosk_ref[...]
    sin_k = sink_ref[...]

    heads = range(RET_HEADS)
    sls = [slice(hd * RET_DK, (hd + 1) * RET_DK) for hd in heads]
    scores, kvs, cross_lhs, out_rhs = [], [], [], []
    for hd in heads:
        q = q_ref[:, sls[hd]]
        q = q * cos + pltpu.roll(q, RET_DK // 2, 1) * sin_signed
        k = k_ref[:, sls[hd]]
        k = k * cos_k + pltpu.roll(k, RET_DK // 2, 1) * sin_k
        vb = v_ref[:, sls[hd]].astype(BF16)
        scores.append(lax.dot_general(q.astype(BF16), k.astype(BF16), (((1,), (1,)), ((), ())),
                                      preferred_element_type=F32))
        kvs.append(lax.dot_general((k * zeta_ref[hd]).astype(BF16), vb, (((0,), (0,)), ((), ())),
                                   preferred_element_type=F32))
        cross_lhs.append((q * xi_ref[hd]).astype(BF16))
        out_rhs.append(jnp.concatenate([vb, state_ref[hd].astype(BF16)], axis=0))
    outs = []
    for hd in heads:
        lhs = jnp.concatenate([(scores[hd] * decay_ref[hd]).astype(BF16), cross_lhs[hd]], axis=1)
        outs.append(jnp.dot(lhs, out_rhs[hd], preferred_element_type=F32))
        state_ref[hd] = math.exp(RET_LOG_GAMMA[hd] * CHUNK) * state_ref[hd] + kvs[hd]
    for hd in heads:
        gate = g_ref[:, sls[hd]]
        y = _head_norm(outs[hd], gn_ref[:, sls[hd]]) * (gate * jax.nn.sigmoid(gate))
        o_ref[:, sls[hd]] = y.astype(BF16)


def _mlstm_kernel(qk_ref, v_ref, og_ref, gates_ref, gbias_ref, cw_ref, cb_ref, gn_ref,
                  c0_ref, n0_ref, m0_ref, tail0_ref,
                  o_ref, c_ref, n_ref, m_ref, tail_ref, xe_ref, *, mask_pad):
    hist = CONV_HIST

    @pl.when(pl.program_id(1) == 0)
    def _():
        xe_ref[0:hist, :] = tail0_ref[...]
        c_ref[...] = c0_ref[...]
        n_ref[...] = n0_ref[...]
        m_ref[...] = m0_ref[...]

    xe_ref[hist:hist + CHUNK, :] = qk_ref[...]
    cw = cw_ref[...]
    y = cb_ref[...]
    for tap in range(MLSTM_CONV):
        off = hist - (MLSTM_CONV - 1) + tap
        y = y + cw[tap:tap + 1, :] * xe_ref[off:off + CHUNK, :]
    tail = xe_ref[CHUNK:CHUNK + hist, :]
    xe_ref[0:hist, :] = tail
    tail_ref[...] = tail
    qk = y * jax.nn.sigmoid(y)

    lane = lax.broadcasted_iota(jnp.int32, (CHUNK, LANES), 1)
    rowi = lax.broadcasted_iota(jnp.int32, (CHUNK, LANES), 0)
    pre = gates_ref[...] + gbias_ref[...]
    log_f = jnp.minimum(pre, 0.0) - jnp.log1p(jnp.exp(-jnp.abs(pre)))
    is_f = jnp.logical_and(lane >= MLSTM_HEADS, lane < N_GATES)
    log_i = pre
    if mask_pad:
        valid = rowi >= PAD
        is_f = jnp.logical_and(is_f, valid)
        log_i = jnp.where(valid, pre, NEG_INF)
    log_f = jnp.where(is_f, log_f, 0.0)
    tri = (rowi >= lane).astype(F32)
    b_cols = jnp.dot(tri, log_f, precision=lax.Precision.HIGHEST, preferred_element_type=F32)
    b_rows = b_cols.T
    log_i_rows = log_i.T
    causal = rowi >= lane

    for hd in range(MLSTM_HEADS):
        q = qk[:, hd * MLSTM_DK:(hd + 1) * MLSTM_DK] * (MLSTM_DK ** -0.5)
        k = qk[:, MLSTM_HEADS * MLSTM_DK + hd * MLSTM_DK:MLSTM_HEADS * MLSTM_DK + (hd + 1) * MLSTM_DK]
        vsl = slice(hd * MLSTM_DV, (hd + 1) * MLSTM_DV)
        vb = v_ref[:, vsl].astype(BF16)
        qb = q.astype(BF16)
        fcol = MLSTM_HEADS + hd
        b_col = b_cols[:, fcol:fcol + 1]
        b_row = b_rows[fcol:fcol + 1, :]
        li_col = log_i[:, hd:hd + 1]
        li_row = log_i_rows[hd:hd + 1, :]
        g_tot = b_cols[CHUNK - 1:CHUNK, fcol:fcol + 1]
        m_prev = m_ref[hd][0:1, 0:1]
        c_prev = c_ref[hd]
        n_prev = n_ref[hd][0:1, :]

        dmat = jnp.where(causal, b_col - b_row + li_row, NEG_INF)
        m_intra = jnp.max(dmat, axis=-1, keepdims=True)
        a_col = g_tot - b_col + li_col
        m_loc = jnp.max(a_col, axis=0, keepdims=True)
        kw = k * jnp.exp(a_col - m_loc)
        kv_loc = lax.dot_general(kw.astype(BF16), vb, (((0,), (0,)), ((), ())),
                                 preferred_element_type=F32)
        n_loc = jnp.sum(kw, axis=0, keepdims=True)

        m_t = jnp.maximum(b_col + m_prev, m_intra)
        s_inter = jnp.exp(b_col + m_prev - m_t)
        p = lax.dot_general(qb, k.astype(BF16), (((1,), (1,)), ((), ())),
                            preferred_element_type=F32) * jnp.exp(dmat - m_t)
        num = jnp.dot(p.astype(BF16), vb, preferred_element_type=F32) \
            + s_inter * jnp.dot(qb, c_prev.astype(BF16), preferred_element_type=F32)
        den = jnp.sum(p, axis=-1, keepdims=True) \
            + s_inter * jnp.sum(q * n_prev, axis=-1, keepdims=True)
        hout = num / jnp.maximum(jnp.abs(den), jnp.exp(-m_t))

        m_new = jnp.maximum(g_tot + m_prev, m_loc)
        s_old = jnp.exp(g_tot + m_prev - m_new)
        s_loc = jnp.exp(m_loc - m_new)
        c_ref[hd] = s_old * c_prev + s_loc * kv_loc
        n_ref[hd] = jnp.broadcast_to(s_old * n_prev + s_loc * n_loc, (SUBLANES, MLSTM_DK))
        m_ref[hd] = jnp.broadcast_to(m_new, (SUBLANES, LANES))

        og = og_ref[:, vsl]
        o_ref[:, vsl] = (_head_norm(hout, gn_ref[:, vsl]) * jax.nn.sigmoid(og)).astype(BF16)


def _mlstm_state_shapes():
    return [jax.ShapeDtypeStruct((MLSTM_HEADS, MLSTM_DK, MLSTM_DV), F32),
            jax.ShapeDtypeStruct((MLSTM_HEADS, SUBLANES, MLSTM_DK), F32),
            jax.ShapeDtypeStruct((MLSTM_HEADS, SUBLANES, LANES), F32),
            jax.ShapeDtypeStruct((CONV_HIST, MLSTM_QK), F32)]


N_RET_IN = 10
N_MLSTM_IN = 12


def _mixers_kernel(*refs, mask_pad):
    ret_in = refs[:N_RET_IN]
    mlstm_in = refs[N_RET_IN:N_RET_IN + N_MLSTM_IN]
    mix_ref, ret_state, c_ref, n_ref, m_ref, tail_ref, decay_ref, zeta_ref, xi_ref, xe_ref = \
        refs[N_RET_IN + N_MLSTM_IN:]
    _retention_kernel(*ret_in, mix_ref.at[:, 0:RET_WIDTH], ret_state, decay_ref, zeta_ref, xi_ref)
    _mlstm_kernel(*mlstm_in, mix_ref.at[:, RET_WIDTH:RET_WIDTH + MLSTM_WIDTH], c_ref, n_ref, m_ref, tail_ref,
                  xe_ref, mask_pad=mask_pad)


def _mixers(proj, gates, rope, ret_init, mlstm_init, nseq, nchunk, mask_pad, *,
            ret_gain, gate_bias, conv_w, conv_b, mlstm_gain):
    m = proj.shape[0]
    qk_blk = 4 * RET_WIDTH // MLSTM_QK
    v_blk = (4 * RET_WIDTH + MLSTM_QK) // MLSTM_WIDTH
    rows = lambda b, n: b * nchunk + n
    const = lambda shape: pl.BlockSpec(shape, lambda b, n: (0,) * len(shape))
    ret_blk = lambda c: pl.BlockSpec((CHUNK, RET_WIDTH), lambda b, n: (rows(b, n), c))
    tab = pl.BlockSpec((CHUNK, RET_DK), lambda b, n: (n, 0))
    state_shapes = [jax.ShapeDtypeStruct((RET_HEADS, RET_DK, RET_DV), F32)] + _mlstm_state_shapes()
    state_specs = [const(s.shape) for s in state_shapes]
    ret_specs = [ret_blk(0), ret_blk(1), ret_blk(2), ret_blk(3), tab, tab, tab, tab,
                 const((1, RET_WIDTH)), state_specs[0]]
    mlstm_specs = [
        pl.BlockSpec((CHUNK, MLSTM_QK), lambda b, n: (rows(b, n), qk_blk)),
        pl.BlockSpec((CHUNK, MLSTM_WIDTH), lambda b, n: (rows(b, n), v_blk)),
        pl.BlockSpec((CHUNK, MLSTM_WIDTH), lambda b, n: (rows(b, n), v_blk + 1)),
        pl.BlockSpec((CHUNK, LANES), lambda b, n: (rows(b, n), 0)),
        const((1, LANES)),
        const((MLSTM_CONV, MLSTM_QK)),
        const((1, MLSTM_QK)),
        const((1, MLSTM_WIDTH)),
    ] + state_specs[1:]
    assert len(ret_specs) == N_RET_IN and len(mlstm_specs) == N_MLSTM_IN
    return pl.pallas_call(
        functools.partial(_mixers_kernel, mask_pad=mask_pad),
        grid=(nseq, nchunk),
        in_specs=ret_specs + mlstm_specs,
        out_specs=[pl.BlockSpec((CHUNK, D_MIX), lambda b, n: (rows(b, n), 0))] + state_specs,
        out_shape=[jax.ShapeDtypeStruct((m, D_MIX), BF16)] + state_shapes,
        scratch_shapes=[pltpu.VMEM((RET_HEADS, CHUNK, CHUNK), F32)] * 3
        + [pltpu.VMEM((CHUNK + 2 * SUBLANES, MLSTM_QK), F32)],
        compiler_params=_params("arbitrary", "arbitrary"),
        name="mixers",
    )(proj, proj, proj, proj, *rope, ret_gain, ret_init,
      proj, proj, proj, gates, gate_bias, conv_w, conv_b, mlstm_gain, *mlstm_init)


def _outproj_kernel(mix_ref, h_ref, w_ref, o_ref, *, is_meta):
    acc = h_ref[...] + jnp.dot(mix_ref[...], w_ref[...], preferred_element_type=F32)
    o_ref[...] = _zero_pad_rows(acc) if is_meta else acc


def _outproj(mix, h, w_out_bf16, layer, is_meta):
    m = h.shape[0]
    tm = min(TM_OUT, m)
    assert not is_meta or m == CHUNK
    return pl.pallas_call(
        functools.partial(_outproj_kernel, is_meta=is_meta),
        grid=(m // tm,),
        in_specs=[
            pl.BlockSpec((tm, D_MIX), lambda i: (i, 0)),
            pl.BlockSpec((tm, D_MODEL), lambda i: (i, 0)),
            pl.BlockSpec((None, D_MIX, D_MODEL), lambda i: (layer, 0, 0)),
        ],
        out_specs=pl.BlockSpec((tm, D_MODEL), lambda i: (i, 0)),
        out_shape=jax.ShapeDtypeStruct((m, D_MODEL), F32),
        compiler_params=_params("parallel"),
        name="outproj",
    )(mix, h, w_out_bf16)


def _ffn_kernel(h_ref, halo_ref, meta_halo_ref, g_ref, wg_ref, wv_ref, cp_ref,
                wd_ref, fg_ref, o_ref, u_ref, upg_ref, upv_ref, *, tiles_per_seq, is_meta, final_norm):
    f = pl.program_id(1)
    halo = BF16_ROWS
    tm = h_ref.shape[0]

    @pl.when(f == 0)
    def _():
        gain = g_ref[...]
        starts_seq = jnp.logical_and(not is_meta, pl.program_id(0) % tiles_per_seq == 0)

        @pl.when(starts_seq)
        def _():
            _rmsnorm_rows(meta_halo_ref, gain, u_ref, 0, halo)

        @pl.when(jnp.logical_not(starts_seq))
        def _():
            _rmsnorm_rows(halo_ref, gain, u_ref, 0, halo)

        _rmsnorm_rows(h_ref, gain, u_ref, halo, tm)
        o_ref[...] = h_ref[...]

    u = u_ref[...]

    upg_ref[...] = jnp.dot(u, wg_ref[...], preferred_element_type=F32)
    upv_ref[...] = jnp.dot(u, wv_ref[...], preferred_element_type=F32)

    cp = cp_ref[...]

    def conv(up_ref, base):
        y = cp[base + FFN_CONV:base + FFN_CONV + 1, :]
        for tap in range(FFN_CONV):
            off = halo - (FFN_CONV - 1) + tap
            y = y + cp[base + tap:base + tap + 1, :] * up_ref[off:off + tm, :]
        return y

    gate = conv(upg_ref, 0)
    val = conv(upv_ref, FFN_CONV + 1)
    act = (gate * jax.nn.sigmoid(gate) * val).astype(BF16)
    o_ref[...] += jnp.dot(act, wd_ref[...], preferred_element_type=F32)

    if is_meta or final_norm:
        @pl.when(f == pl.num_programs(1) - 1)
        def _():
            if is_meta:
                o_ref[...] = _zero_pad_rows(o_ref[...])
            if final_norm:
                _rmsnorm_rows(o_ref, fg_ref[...], o_ref, 0, tm)


def _ffn(h, meta_h, gain, w_up, conv_pack, w_down, layer, final_gain, seq, is_meta, final_norm):
    m = h.shape[0]
    tm = min(TM_FFN, m)
    nf = FFN_DIM // TF_FFN
    halo_blocks = tm // BF16_ROWS
    assert not is_meta or m == CHUNK
    return pl.pallas_call(
        functools.partial(_ffn_kernel, tiles_per_seq=seq // tm, is_meta=is_meta, final_norm=final_norm),
        grid=(m // tm, nf),
        in_specs=[
            pl.BlockSpec((tm, D_MODEL), lambda i, f: (i, 0)),
            pl.BlockSpec((BF16_ROWS, D_MODEL), lambda i, f: (jnp.maximum(i * halo_blocks - 1, 0), 0)),
            pl.BlockSpec((BF16_ROWS, D_MODEL), lambda i, f: (CHUNK // BF16_ROWS - 1, 0)),
            pl.BlockSpec((1, D_MODEL), lambda i, f: (0, 0)),
            pl.BlockSpec((None, D_MODEL, TF_FFN), lambda i, f: (layer, 0, f)),
            pl.BlockSpec((None, D_MODEL, TF_FFN), lambda i, f: (layer, 0, nf + f)),
            pl.BlockSpec((None, None, 2 * (FFN_CONV + 1), TF_FFN), lambda i, f: (layer, f, 0, 0)),
            pl.BlockSpec((None, TF_FFN, D_MODEL), lambda i, f: (layer, f, 0)),
            pl.BlockSpec((1, D_MODEL), lambda i, f: (0, 0)),
        ],
        out_specs=pl.BlockSpec((tm, D_MODEL), lambda i, f: (i, 0)),
        out_shape=jax.ShapeDtypeStruct((m, D_MODEL), F32),
        scratch_shapes=[
            pltpu.VMEM((tm + BF16_ROWS, D_MODEL), BF16),
            pltpu.VMEM((tm + BF16_ROWS, TF_FFN), F32),
            pltpu.VMEM((tm + BF16_ROWS, TF_FFN), F32),
        ],
        compiler_params=_params("parallel", "arbitrary"),
        name="convffn",
    )(h, h, meta_h, gain, w_up, w_up, conv_pack, w_down, final_gain)


def _pack_ffn_conv(conv_w, conv_b):
    depth = conv_w.shape[0]
    nf = FFN_DIM // TF_FFN
    w = conv_w.reshape(depth, FFN_CONV, 2, nf, TF_FFN)
    b = conv_b.reshape(depth, 1, 2, nf, TF_FFN)
    rows = jnp.concatenate([w[:, :, 0], b[:, :, 0], w[:, :, 1], b[:, :, 1]], axis=1)
    return jnp.swapaxes(rows, 1, 2)


def _rope_tables(pos):
    inv = ROPE_THETA ** (-jnp.arange(0, RET_DK, 2, dtype=F32) / RET_DK)
    ang = pos[:, None] * inv[None, :]
    cos = jnp.concatenate([jnp.cos(ang)] * 2, axis=-1)
    sin_signed = jnp.concatenate([-jnp.sin(ang), jnp.sin(ang)], axis=-1)
    return cos, sin_signed, cos * RET_DK ** -0.5, sin_signed * RET_DK ** -0.5


def kernel(x, meta_tokens, norm_mix, w_in, mlstm_conv_w, mlstm_conv_b, mlstm_b_i, mlstm_b_f, ret_norm, mlstm_norm, w_out, norm_ffn, w_up, ffn_conv_w, ffn_conv_b, w_down, norm_final):
    batch, seq, d = x.shape
    depth = w_in.shape[0]
    assert d == D_MODEL and w_in.shape[2] == D_MAIN + N_GATES and meta_tokens.shape[0] == N_META
    assert seq % TM_FFN == 0 and (batch * seq) % TM_PROJ == 0

    h = x.reshape(batch * seq, d)
    h_meta = jnp.concatenate([jnp.zeros((PAD, d), x.dtype), meta_tokens.astype(x.dtype)], axis=0)
    rope_meta = _rope_tables(jnp.maximum(jnp.arange(CHUNK, dtype=F32) - PAD, 0.0))
    rope_seq = _rope_tables(jnp.arange(N_META, N_META + seq, dtype=F32))
    ret_zero = jnp.zeros((RET_HEADS, RET_DK, RET_DV), F32)
    mlstm_zero = [jnp.zeros(s.shape, s.dtype) for s in _mlstm_state_shapes()]

    w_in_b, w_out_b, w_up_b, w_down_b = (w.astype(BF16) for w in (w_in, w_out, w_up, w_down))
    conv_pack = _pack_ffn_conv(ffn_conv_w, ffn_conv_b)
    final_gain = norm_final[None, :]

    for l in range(depth):
        last = l == depth - 1
        w_gate = jnp.pad(w_in[l, :, D_MAIN:], ((0, 0), (0, LANES - N_GATES))).astype(BF16)
        gate_bias = jnp.pad(jnp.concatenate([mlstm_b_i[l], mlstm_b_f[l]]), (0, LANES - N_GATES))[None, :]
        mix_gain = norm_mix[l][None, :]
        mixers = functools.partial(_mixers, ret_gain=ret_norm[l][None, :], gate_bias=gate_bias,
                                   conv_w=mlstm_conv_w[l], conv_b=mlstm_conv_b[l][None, :],
                                   mlstm_gain=mlstm_norm[l][None, :])
        ffn_args = (norm_ffn[l][None, :], w_up_b, conv_pack, w_down_b, l, final_gain, seq)

        proj, gates = _inproj(h_meta, mix_gain, w_in_b, l, w_gate)
        mix, ret_state, *mlstm_state = mixers(proj, gates, rope_meta, ret_zero, mlstm_zero, 1, 1, True)
        h_meta = _outproj(mix, h_meta, w_out_b, l, True)
        meta_ffn_in = h_meta
        if not last:
            h_meta = _ffn(h_meta, meta_ffn_in, *ffn_args, True, False)

        proj, gates = _inproj(h, mix_gain, w_in_b, l, w_gate)
        mix, *_ = mixers(proj, gates, rope_seq, ret_state, mlstm_state, batch, seq // CHUNK, False)
        h = _outproj(mix, h, w_out_b, l, False)
        h = _ffn(h, meta_ffn_in, *ffn_args, False, last)

    return h.reshape(batch, seq, d)
```

```python
import functools
import math

import jax
import jax.numpy as jnp
from jax import lax
from jax.experimental import pallas as pl
from jax.experimental.pallas import tpu as pltpu

F32 = jnp.float32
BF16 = jnp.bfloat16

D_MODEL = 2048
N_META = 16
CHUNK = 128
PAD = CHUNK - N_META
RET_HEADS = 8
RET_DK = 128
RET_DV = 128
RET_WIDTH = RET_HEADS * RET_DV
MLSTM_HEADS = 4
MLSTM_DK = 128
MLSTM_DV = 256
MLSTM_WIDTH = MLSTM_HEADS * MLSTM_DV
D_MIX = RET_WIDTH + MLSTM_WIDTH
MLSTM_QK = 2 * MLSTM_HEADS * MLSTM_DK
MLSTM_CONV = 4
FFN_DIM = 5632
FFN_CONV = 3
ROPE_THETA = 10000.0
EPS = 1e-6
D_MAIN = 4 * RET_WIDTH + MLSTM_QK + 2 * MLSTM_WIDTH
N_GATES = 2 * MLSTM_HEADS

LANES = 128
SUBLANES = 8
BF16_ROWS = 16
VMEM_LIMIT = 60 * 1024 * 1024

TM_PROJ = 1024
TN_PROJ = 1792
TM_OUT = 512
TM_FFN = 1024
TF_FFN = 512
RMS_ROWS = 128
CONV_HIST = SUBLANES

RET_LOG_GAMMA = [math.log1p(-(2.0 ** (-5.0 - h))) for h in range(RET_HEADS)]
NEG_INF = float("-inf")


def _params(*sem):
    return pltpu.CompilerParams(dimension_semantics=sem, vmem_limit_bytes=VMEM_LIMIT)


def _rmsnorm_rows(src_ref, gain, dst_ref, dst_off, nrows):
    blk = min(RMS_ROWS, nrows)

    def body(r, carry):
        r0 = pl.multiple_of(r * blk, blk)
        x = src_ref[pl.ds(r0, blk), :]
        ms = jnp.mean(x * x, axis=-1, keepdims=True)
        dst_ref[pl.ds(dst_off + r0, blk), :] = (x * lax.rsqrt(ms + EPS) * gain).astype(dst_ref.dtype)
        return carry

    lax.fori_loop(0, nrows // blk, body, 0)


def _zero_pad_rows(x):
    row = lax.broadcasted_iota(jnp.int32, (CHUNK, 1), 0)
    return jnp.where(row >= PAD, x, 0.0)


def _inproj_kernel(h_ref, g_ref, w_ref, wg_ref, proj_ref, gates_ref, u_ref):
    @pl.when(pl.program_id(1) == 0)
    def _():
        _rmsnorm_rows(h_ref, g_ref[...], u_ref, 0, h_ref.shape[0])
        gates_ref[...] = jnp.dot(u_ref[...], wg_ref[...], preferred_element_type=F32)

    proj_ref[...] = jnp.dot(u_ref[...], w_ref[...], preferred_element_type=F32)


def _inproj(h, gain, w_in_bf16, layer, w_gate):
    m = h.shape[0]
    tm = min(TM_PROJ, m)
    return pl.pallas_call(
        _inproj_kernel,
        grid=(m // tm, D_MAIN // TN_PROJ),
        in_specs=[
            pl.BlockSpec((tm, D_MODEL), lambda i, j: (i, 0)),
            pl.BlockSpec((1, D_MODEL), lambda i, j: (0, 0)),
            pl.BlockSpec((None, D_MODEL, TN_PROJ), lambda i, j: (layer, 0, j)),
            pl.BlockSpec((D_MODEL, LANES), lambda i, j: (0, 0)),
        ],
        out_specs=[
            pl.BlockSpec((tm, TN_PROJ), lambda i, j: (i, j)),
            pl.BlockSpec((tm, LANES), lambda i, j: (i, 0)),
        ],
        out_shape=[
            jax.ShapeDtypeStruct((m, D_MAIN), F32),
            jax.ShapeDtypeStruct((m, LANES), F32),
        ],
        scratch_shapes=[pltpu.VMEM((tm, D_MODEL), BF16)],
        compiler_params=_params("parallel", "arbitrary"),
        name="inproj",
    )(h, gain, w_in_bf16, w_gate)


def _head_norm(x, gain):
    mu = jnp.mean(x, axis=-1, keepdims=True)
    xc = x - mu
    return xc * lax.rsqrt(jnp.mean(xc * xc, axis=-1, keepdims=True) + EPS) * gain


def _retention_kernel(q_ref, k_ref, v_ref, g_ref, cos_ref, sin_ref, cosk_ref, sink_ref, gn_ref, init_ref,
                      o_ref, state_ref, decay_ref, zeta_ref, xi_ref):
    @pl.when(jnp.logical_and(pl.program_id(0) == 0, pl.program_id(1) == 0))
    def _():
        row = lax.broadcasted_iota(jnp.int32, (CHUNK, CHUNK), 0)
        col = lax.broadcasted_iota(jnp.int32, (CHUNK, CHUNK), 1)
        rel = row - col
        rel_f = jnp.maximum(rel, 0).astype(F32)
        row_f = row.astype(F32)
        for hd in range(RET_HEADS):
            lg = RET_LOG_GAMMA[hd]
            decay_ref[hd] = jnp.where(rel >= 0, jnp.exp(lg * rel_f), 0.0)
            zeta_ref[hd] = jnp.exp(lg * (CHUNK - 1.0 - row_f))
            xi_ref[hd] = jnp.exp(lg * (row_f + 1.0))

    @pl.when(pl.program_id(1) == 0)
    def _():
        state_ref[...] = init_ref[...]

    cos = cos_ref[...]
    sin_signed = sin_ref[...]
    cos_k = cosk_ref[...]
    sin_k = sink_ref[...]

    heads = range(RET_HEADS)
    sls = [slice(hd * RET_DK, (hd + 1) * RET_DK) for hd in heads]
    scores, kvs, cross_lhs, out_rhs = [], [], [], []
    for hd in heads:
        q = q_ref[:, sls[hd]]
        q = q * cos + pltpu.roll(q, RET_DK // 2, 1) * sin_signed
        k = k_ref[:, sls[hd]]
        k = k * cos_k + pltpu.roll(k, RET_DK // 2, 1) * sin_k
        vb = v_ref[:, sls[hd]].astype(BF16)
        scores.append(lax.dot_general(q.astype(BF16), k.astype(BF16), (((1,), (1,)), ((), ())),
                                      preferred_element_type=F32))
        kvs.append(lax.dot_general((k * zeta_ref[hd]).astype(BF16), vb, (((0,), (0,)), ((), ())),
                                   preferred_element_type=F32))
        cross_lhs.append((q * xi_ref[hd]).astype(BF16))
        out_rhs.append(jnp.concatenate([vb, state_ref[hd].astype(BF16)], axis=0))
    outs = []
    for hd in heads:
        lhs = jnp.concatenate([(scores[hd] * decay_ref[hd]).astype(BF16), cross_lhs[hd]], axis=1)
        outs.append(jnp.dot(lhs, out_rhs[hd], preferred_element_type=F32))
        state_ref[hd] = math.exp(RET_LOG_GAMMA[hd] * CHUNK) * state_ref[hd] + kvs[hd]
    for hd in heads:
        gate = g_ref[:, sls[hd]]
        y = _head_norm(outs[hd], gn_ref[:, sls[hd]]) * (gate * jax.nn.sigmoid(gate))
        o_ref[:, sls[hd]] = y.astype(BF16)


def _mlstm_kernel(qk_ref, v_ref, og_ref, gates_ref, gbias_ref, cw_ref, cb_ref, gn_ref,
                  c0_ref, n0_ref, m0_ref, tail0_ref,
                  o_ref, c_ref, n_ref, m_ref, tail_ref, xe_ref, *, mask_pad):
    hist = CONV_HIST

    @pl.when(pl.program_id(1) == 0)
    def _():
        xe_ref[0:hist, :] = tail0_ref[...]
        c_ref[...] = c0_ref[...]
        n_ref[...] = n0_ref[...]
        m_ref[...] = m0_ref[...]

    xe_ref[hist:hist + CHUNK, :] = qk_ref[...]
    cw = cw_ref[...]
    y = cb_ref[...]
    for tap in range(MLSTM_CONV):
        off = hist - (MLSTM_CONV - 1) + tap
        y = y + cw[tap:tap + 1, :] * xe_ref[off:off + CHUNK, :]
    tail = xe_ref[CHUNK:CHUNK + hist, :]
    xe_ref[0:hist, :] = tail
    tail_ref[...] = tail
    qk = y * jax.nn.sigmoid(y)

    lane = lax.broadcasted_iota(jnp.int32, (CHUNK, LANES), 1)
    rowi = lax.broadcasted_iota(jnp.int32, (CHUNK, LANES), 0)
    pre = gates_ref[...] + gbias_ref[...]
    log_f = jnp.minimum(pre, 0.0) - jnp.log1p(jnp.exp(-jnp.abs(pre)))
    is_f = jnp.logical_and(lane >= MLSTM_HEADS, lane < N_GATES)
    log_i = pre
    if mask_pad:
        valid = rowi >= PAD
        is_f = jnp.logical_and(is_f, valid)
        log_i = jnp.where(valid, pre, NEG_INF)
    log_f = jnp.where(is_f, log_f, 0.0)
    tri = (rowi >= lane).astype(F32)
    b_cols = jnp.dot(tri, log_f, precision=lax.Precision.HIGHEST, preferred_element_type=F32)
    b_rows = b_cols.T
    log_i_rows = log_i.T
    causal = rowi >= lane

    for hd in range(MLSTM_HEADS):
        q = qk[:, hd * MLSTM_DK:(hd + 1) * MLSTM_DK] * (MLSTM_DK ** -0.5)
        k = qk[:, MLSTM_HEADS * MLSTM_DK + hd * MLSTM_DK:MLSTM_HEADS * MLSTM_DK + (hd + 1) * MLSTM_DK]
        vsl = slice(hd * MLSTM_DV, (hd + 1) * MLSTM_DV)
        vb = v_ref[:, vsl].astype(BF16)
        qb = q.astype(BF16)
        fcol = MLSTM_HEADS + hd
        b_col = b_cols[:, fcol:fcol + 1]
        b_row = b_rows[fcol:fcol + 1, :]
        li_col = log_i[:, hd:hd + 1]
        li_row = log_i_rows[hd:hd + 1, :]
        g_tot = b_cols[CHUNK - 1:CHUNK, fcol:fcol + 1]
        m_prev = m_ref[hd][0:1, 0:1]
        c_prev = c_ref[hd]
        n_prev = n_ref[hd][0:1, :]

        dmat = jnp.where(causal, b_col - b_row + li_row, NEG_INF)
        m_intra = jnp.max(dmat, axis=-1, keepdims=True)
        a_col = g_tot - b_col + li_col
        m_loc = jnp.max(a_col, axis=0, keepdims=True)
        kw = k * jnp.exp(a_col - m_loc)
        kv_loc = lax.dot_general(kw.astype(BF16), vb, (((0,), (0,)), ((), ())),
                                 preferred_element_type=F32)
        n_loc = jnp.sum(kw, axis=0, keepdims=True)

        m_t = jnp.maximum(b_col + m_prev, m_intra)
        s_inter = jnp.exp(b_col + m_prev - m_t)
        p = lax.dot_general(qb, k.astype(BF16), (((1,), (1,)), ((), ())),
                            preferred_element_type=F32) * jnp.exp(dmat - m_t)
        num = jnp.dot(p.astype(BF16), vb, preferred_element_type=F32) \
            + s_inter * jnp.dot(qb, c_prev.astype(BF16), preferred_element_type=F32)
        den = jnp.sum(p, axis=-1, keepdims=True) \
            + s_inter * jnp.sum(q * n_prev, axis=-1, keepdims=True)
        hout = num / jnp.maximum(jnp.abs(den), jnp.exp(-m_t))

        m_new = jnp.maximum(g_tot + m_prev, m_loc)
        s_old = jnp.exp(g_tot + m_prev - m_new)
        s_loc = jnp.exp(m_loc - m_new)
        c_ref[hd] = s_old * c_prev + s_loc * kv_loc
        n_ref[hd] = jnp.broadcast_to(s_old * n_prev + s_loc * n_loc, (SUBLANES, MLSTM_DK))
        m_ref[hd] = jnp.broadcast_to(m_new, (SUBLANES, LANES))

        og = og_ref[:, vsl]
        o_ref[:, vsl] = (_head_norm(hout, gn_ref[:, vsl]) * jax.nn.sigmoid(og)).astype(BF16)


def _mlstm_state_shapes():
    return [jax.ShapeDtypeStruct((MLSTM_HEADS, MLSTM_DK, MLSTM_DV), F32),
            jax.ShapeDtypeStruct((MLSTM_HEADS, SUBLANES, MLSTM_DK), F32),
            jax.ShapeDtypeStruct((MLSTM_HEADS, SUBLANES, LANES), F32),
            jax.ShapeDtypeStruct((CONV_HIST, MLSTM_QK), F32)]


N_RET_IN = 10
N_MLSTM_IN = 12


def _mixers_kernel(*refs, mask_pad):
    ret_in = refs[:N_RET_IN]
    mlstm_in = refs[N_RET_IN:N_RET_IN + N_MLSTM_IN]
    mix_ref, ret_state, c_ref, n_ref, m_ref, tail_ref, decay_ref, zeta_ref, xi_ref, xe_ref = \
        refs[N_RET_IN + N_MLSTM_IN:]
    _retention_kernel(*ret_in, mix_ref.at[:, 0:RET_WIDTH], ret_state, decay_ref, zeta_ref, xi_ref)
    _mlstm_kernel(*mlstm_in, mix_ref.at[:, RET_WIDTH:RET_WIDTH + MLSTM_WIDTH], c_ref, n_ref, m_ref, tail_ref,
                  xe_ref, mask_pad=mask_pad)


def _mixers(proj, gates, rope, ret_init, mlstm_init, nseq, nchunk, mask_pad, *,
            ret_gain, gate_bias, conv_w, conv_b, mlstm_gain):
    m = proj.shape[0]
    qk_blk = 4 * RET_WIDTH // MLSTM_QK
    v_blk = (4 * RET_WIDTH + MLSTM_QK) // MLSTM_WIDTH
    rows = lambda b, n: b * nchunk + n
    const = lambda shape: pl.BlockSpec(shape, lambda b, n: (0,) * len(shape))
    ret_blk = lambda c: pl.BlockSpec((CHUNK, RET_WIDTH), lambda b, n: (rows(b, n), c))
    tab = pl.BlockSpec((CHUNK, RET_DK), lambda b, n: (n, 0))
    state_shapes = [jax.ShapeDtypeStruct((RET_HEADS, RET_DK, RET_DV), F32)] + _mlstm_state_shapes()
    state_specs = [const(s.shape) for s in state_shapes]
    ret_specs = [ret_blk(0), ret_blk(1), ret_blk(2), ret_blk(3), tab, tab, tab, tab,
                 const((1, RET_WIDTH)), state_specs[0]]
    mlstm_specs = [
        pl.BlockSpec((CHUNK, MLSTM_QK), lambda b, n: (rows(b, n), qk_blk)),
        pl.BlockSpec((CHUNK, MLSTM_WIDTH), lambda b, n: (rows(b, n), v_blk)),
        pl.BlockSpec((CHUNK, MLSTM_WIDTH), lambda b, n: (rows(b, n), v_blk + 1)),
        pl.BlockSpec((CHUNK, LANES), lambda b, n: (rows(b, n), 0)),
        const((1, LANES)),
        const((MLSTM_CONV, MLSTM_QK)),
        const((1, MLSTM_QK)),
        const((1, MLSTM_WIDTH)),
    ] + state_specs[1:]
    assert len(ret_specs) == N_RET_IN and len(mlstm_specs) == N_MLSTM_IN
    return pl.pallas_call(
        functools.partial(_mixers_kernel, mask_pad=mask_pad),
        grid=(nseq, nchunk),
        in_specs=ret_specs + mlstm_specs,
        out_specs=[pl.BlockSpec((CHUNK, D_MIX), lambda b, n: (rows(b, n), 0))] + state_specs,
        out_shape=[jax.ShapeDtypeStruct((m, D_MIX), BF16)] + state_shapes,
        scratch_shapes=[pltpu.VMEM((RET_HEADS, CHUNK, CHUNK), F32)] * 3
        + [pltpu.VMEM((CHUNK + 2 * SUBLANES, MLSTM_QK), F32)],
        compiler_params=_params("arbitrary", "arbitrary"),
        name="mixers",
    )(proj, proj, proj, proj, *rope, ret_gain, ret_init,
      proj, proj, proj, gates, gate_bias, conv_w, conv_b, mlstm_gain, *mlstm_init)


def _outproj_kernel(mix_ref, h_ref, w_ref, o_ref, *, is_meta):
    acc = h_ref[...] + jnp.dot(mix_ref[...], w_ref[...], preferred_element_type=F32)
    o_ref[...] = _zero_pad_rows(acc) if is_meta else acc


def _outproj(mix, h, w_out_bf16, layer, is_meta):
    m = h.shape[0]
    tm = min(TM_OUT, m)
    assert not is_meta or m == CHUNK
    return pl.pallas_call(
        functools.partial(_outproj_kernel, is_meta=is_meta),
        grid=(m // tm,),
        in_specs=[
            pl.BlockSpec((tm, D_MIX), lambda i: (i, 0)),
            pl.BlockSpec((tm, D_MODEL), lambda i: (i, 0)),
            pl.BlockSpec((None, D_MIX, D_MODEL), lambda i: (layer, 0, 0)),
        ],
        out_specs=pl.BlockSpec((tm, D_MODEL), lambda i: (i, 0)),
        out_shape=jax.ShapeDtypeStruct((m, D_MODEL), F32),
        compiler_params=_params("parallel"),
        name="outproj",
    )(mix, h, w_out_bf16)


def _ffn_kernel(h_ref, halo_ref, meta_halo_ref, g_ref, wg_ref, wv_ref, cp_ref,
                wd_ref, fg_ref, o_ref, u_ref, upg_ref, upv_ref, *, tiles_per_seq, is_meta, final_norm):
    f = pl.program_id(1)
    halo = BF16_ROWS
    tm = h_ref.shape[0]

    @pl.when(f == 0)
    def _():
        gain = g_ref[...]
        starts_seq = jnp.logical_and(not is_meta, pl.program_id(0) % tiles_per_seq == 0)

        @pl.when(starts_seq)
        def _():
            _rmsnorm_rows(meta_halo_ref, gain, u_ref, 0, halo)

        @pl.when(jnp.logical_not(starts_seq))
        def _():
            _rmsnorm_rows(halo_ref, gain, u_ref, 0, halo)

        _rmsnorm_rows(h_ref, gain, u_ref, halo, tm)
        o_ref[...] = h_ref[...]

    u = u_ref[...]

    upg_ref[...] = jnp.dot(u, wg_ref[...], preferred_element_type=F32)
    upv_ref[...] = jnp.dot(u, wv_ref[...], preferred_element_type=F32)

    cp = cp_ref[...]

    def conv(up_ref, base):
        y = cp[base + FFN_CONV:base + FFN_CONV + 1, :]
        for tap in range(FFN_CONV):
            off = halo - (FFN_CONV - 1) + tap
            y = y + cp[base + tap:base + tap + 1, :] * up_ref[off:off + tm, :]
        return y

    gate = conv(upg_ref, 0)
    val = conv(upv_ref, FFN_CONV + 1)
    act = (gate * jax.nn.sigmoid(gate) * val).astype(BF16)
    o_ref[...] += jnp.dot(act, wd_ref[...], preferred_element_type=F32)

    if is_meta or final_norm:
        @pl.when(f == pl.num_programs(1) - 1)
        def _():
            if is_meta:
                o_ref[...] = _zero_pad_rows(o_ref[...])
            if final_norm:
                _rmsnorm_rows(o_ref, fg_ref[...], o_ref, 0, tm)


def _ffn(h, meta_h, gain, w_up, conv_pack, w_down, layer, final_gain, seq, is_meta, final_norm):
    m = h.shape[0]
    tm = min(TM_FFN, m)
    nf = FFN_DIM // TF_FFN
    halo_blocks = tm // BF16_ROWS
    assert not is_meta or m == CHUNK
    return pl.pallas_call(
        functools.partial(_ffn_kernel, tiles_per_seq=seq // tm, is_meta=is_meta, final_norm=final_norm),
        grid=(m // tm, nf),
        in_specs=[
            pl.BlockSpec((tm, D_MODEL), lambda i, f: (i, 0)),
            pl.BlockSpec((BF16_ROWS, D_MODEL), lambda i, f: (jnp.maximum(i * halo_blocks - 1, 0), 0)),
            pl.BlockSpec((BF16_ROWS, D_MODEL), lambda i, f: (CHUNK // BF16_ROWS - 1, 0)),
            pl.BlockSpec((1, D_MODEL), lambda i, f: (0, 0)),
            pl.BlockSpec((None, D_MODEL, TF_FFN), lambda i, f: (layer, 0, f)),
            pl.BlockSpec((None, D_MODEL, TF_FFN), lambda i, f: (layer, 0, nf + f)),
            pl.BlockSpec((None, None, 2 * (FFN_CONV + 1), TF_FFN), lambda i, f: (layer, f, 0, 0)),
            pl.BlockSpec((None, TF_FFN, D_MODEL), lambda i, f: (layer, f, 0)),
            pl.BlockSpec((1, D_MODEL), lambda i, f: (0, 0)),
        ],
        out_specs=pl.BlockSpec((tm, D_MODEL), lambda i, f: (i, 0)),
        out_shape=jax.ShapeDtypeStruct((m, D_MODEL), F32),
        scratch_shapes=[
            pltpu.VMEM((tm + BF16_ROWS, D_MODEL), BF16),
            pltpu.VMEM((tm + BF16_ROWS, TF_FFN), F32),
            pltpu.VMEM((tm + BF16_ROWS, TF_FFN), F32),
        ],
        compiler_params=_params("parallel", "arbitrary"),
        name="convffn",
    )(h, h, meta_h, gain, w_up, w_up, conv_pack, w_down, final_gain)


def _pack_ffn_conv(conv_w, conv_b):
    depth = conv_w.shape[0]
    nf = FFN_DIM // TF_FFN
    w = conv_w.reshape(depth, FFN_CONV, 2, nf, TF_FFN)
    b = conv_b.reshape(depth, 1, 2, nf, TF_FFN)
    rows = jnp.concatenate([w[:, :, 0], b[:, :, 0], w[:, :, 1], b[:, :, 1]], axis=1)
    return jnp.swapaxes(rows, 1, 2)


def _rope_tables(pos):
    inv = ROPE_THETA ** (-jnp.arange(0, RET_DK, 2, dtype=F32) / RET_DK)
    ang = pos[:, None] * inv[None, :]
    cos = jnp.concatenate([jnp.cos(ang)] * 2, axis=-1)
    sin_signed = jnp.concatenate([-jnp.sin(ang), jnp.sin(ang)], axis=-1)
    return cos, sin_signed, cos * RET_DK ** -0.5, sin_signed * RET_DK ** -0.5


def kernel(x, meta_tokens, norm_mix, w_in, mlstm_conv_w, mlstm_conv_b, mlstm_b_i, mlstm_b_f, ret_norm, mlstm_norm, w_out, norm_ffn, w_up, ffn_conv_w, ffn_conv_b, w_down, norm_final):
    batch, seq, d = x.shape
    depth = w_in.shape[0]
    assert d == D_MODEL and w_in.shape[2] == D_MAIN + N_GATES and meta_tokens.shape[0] == N_META
    assert seq % TM_FFN == 0 and (batch * seq) % TM_PROJ == 0

    h = x.reshape(batch * seq, d)
    h_meta = jnp.concatenate([jnp.zeros((PAD, d), x.dtype), meta_tokens.astype(x.dtype)], axis=0)
    rope_meta = _rope_tables(jnp.maximum(jnp.arange(CHUNK, dtype=F32) - PAD, 0.0))
    rope_seq = _rope_tables(jnp.arange(N_META, N_META + seq, dtype=F32))
    ret_zero = jnp.zeros((RET_HEADS, RET_DK, RET_DV), F32)
    mlstm_zero = [jnp.zeros(s.shape, s.dtype) for s in _mlstm_state_shapes()]

    w_in_b, w_out_b, w_up_b, w_down_b = (w.astype(BF16) for w in (w_in, w_out, w_up, w_down))
    conv_pack = _pack_ffn_conv(ffn_conv_w, ffn_conv_b)
    final_gain = norm_final[None, :]

    for l in range(depth):
        last = l == depth - 1
        w_gate = jnp.pad(w_in[l, :, D_MAIN:], ((0, 0), (0, LANES - N_GATES))).astype(BF16)
        gate_bias = jnp.pad(jnp.concatenate([mlstm_b_i[l], mlstm_b_f[l]]), (0, LANES - N_GATES))[None, :]
        mix_gain = norm_mix[l][None, :]
        mixers = functools.partial(_mixers, ret_gain=ret_norm[l][None, :], gate_bias=gate_bias,
                                   conv_w=mlstm_conv_w[l], conv_b=mlstm_conv_b[l][None, :],
                                   mlstm_gain=mlstm_norm[l][None, :])
        ffn_args = (norm_ffn[l][None, :], w_up_b, conv_pack, w_down_b, l, final_gain, seq)

        proj, gates = _inproj(h_meta, mix_gain, w_in_b, l, w_gate)
        mix, ret_state, *mlstm_state = mixers(proj, gates, rope_meta, ret_zero, mlstm_zero, 1, 1, True)
        h_meta = _outproj(mix, h_meta, w_out_b, l, True)
        meta_ffn_in = h_meta
        if not last:
            h_meta = _ffn(h_meta, meta_ffn_in, *ffn_args, True, False)

        proj, gates = _inproj(h, mix_gain, w_in_b, l, w_gate)
        mix, *_ = mixers(proj, gates, rope_seq, ret_state, mlstm_state, batch, seq // CHUNK, False)
        h = _outproj(mix, h, w_out_b, l, False)
        h = _ffn(h, meta_ffn_in, *ffn_args, False, last)

    return h.reshape(batch, seq, d)
```

```python
import functools
import math

import jax
import jax.numpy as jnp
from jax import lax
from jax.experimental import pallas as pl
from jax.experimental.pallas import tpu as pltpu

F32 = jnp.float32
BF16 = jnp.bfloat16

D_MODEL = 2048
N_META = 16
CHUNK = 128
PAD = CHUNK - N_META
RET_HEADS = 8
RET_DK = 128
RET_DV = 128
RET_WIDTH = RET_HEADS * RET_DV
MLSTM_HEADS = 4
MLSTM_DK = 128
MLSTM_DV = 256
MLSTM_WIDTH = MLSTM_HEADS * MLSTM_DV
D_MIX = RET_WIDTH + MLSTM_WIDTH
MLSTM_QK = 2 * MLSTM_HEADS * MLSTM_DK
MLSTM_CONV = 4
FFN_DIM = 5632
FFN_CONV = 3
ROPE_THETA = 10000.0
EPS = 1e-6
D_MAIN = 4 * RET_WIDTH + MLSTM_QK + 2 * MLSTM_WIDTH
N_GATES = 2 * MLSTM_HEADS

LANES = 128
SUBLANES = 8
BF16_ROWS = 16
VMEM_LIMIT = 60 * 1024 * 1024

TM_PROJ = 1024
TN_PROJ = 1792
TM_OUT = 512
TM_FFN = 1024
TF_FFN = 512
RMS_ROWS = 128
CONV_HIST = SUBLANES

RET_LOG_GAMMA = [math.log1p(-(2.0 ** (-5.0 - h))) for h in range(RET_HEADS)]
NEG_INF = float("-inf")


def _params(*sem):
    return pltpu.CompilerParams(dimension_semantics=sem, vmem_limit_bytes=VMEM_LIMIT)


def _rmsnorm_rows(src_ref, gain, dst_ref, dst_off, nrows):
    blk = min(RMS_ROWS, nrows)

    def body(r, carry):
        r0 = pl.multiple_of(r * blk, blk)
        x = src_ref[pl.ds(r0, blk), :]
        ms = jnp.mean(x * x, axis=-1, keepdims=True)
        dst_ref[pl.ds(dst_off + r0, blk), :] = (x * lax.rsqrt(ms + EPS) * gain).astype(dst_ref.dtype)
        return carry

    lax.fori_loop(0, nrows // blk, body, 0)


def _zero_pad_rows(x):
    row = lax.broadcasted_iota(jnp.int32, (CHUNK, 1), 0)
    return jnp.where(row >= PAD, x, 0.0)


def _inproj_kernel(h_ref, g_ref, w_ref, wg_ref, proj_ref, gates_ref, u_ref):
    @pl.when(pl.program_id(1) == 0)
    def _():
        _rmsnorm_rows(h_ref, g_ref[...], u_ref, 0, h_ref.shape[0])
        gates_ref[...] = jnp.dot(u_ref[...], wg_ref[...], preferred_element_type=F32)

    proj_ref[...] = jnp.dot(u_ref[...], w_ref[...], preferred_element_type=F32)


def _inproj(h, gain, w_in_bf16, layer, w_gate):
    m = h.shape[0]
    tm = min(TM_PROJ, m)
    return pl.pallas_call(
        _inproj_kernel,
        grid=(m // tm, D_MAIN // TN_PROJ),
        in_specs=[
            pl.BlockSpec((tm, D_MODEL), lambda i, j: (i, 0)),
            pl.BlockSpec((1, D_MODEL), lambda i, j: (0, 0)),
            pl.BlockSpec((None, D_MODEL, TN_PROJ), lambda i, j: (layer, 0, j)),
            pl.BlockSpec((D_MODEL, LANES), lambda i, j: (0, 0)),
        ],
        out_specs=[
            pl.BlockSpec((tm, TN_PROJ), lambda i, j: (i, j)),
            pl.BlockSpec((tm, LANES), lambda i, j: (i, 0)),
        ],
        out_shape=[
            jax.ShapeDtypeStruct((m, D_MAIN), F32),
            jax.ShapeDtypeStruct((m, LANES), F32),
        ],
        scratch_shapes=[pltpu.VMEM((tm, D_MODEL), BF16)],
        compiler_params=_params("parallel", "arbitrary"),
        name="inproj",
    )(h, gain, w_in_bf16, w_gate)


def _head_norm(x, gain):
    mu = jnp.mean(x, axis=-1, keepdims=True)
    xc = x - mu
    return xc * lax.rsqrt(jnp.mean(xc * xc, axis=-1, keepdims=True) + EPS) * gain


def _retention_kernel(q_ref, k_ref, v_ref, g_ref, cos_ref, sin_ref, cosk_ref, sink_ref, gn_ref, init_ref,
                      o_ref, state_ref, decay_ref, zeta_ref, xi_ref):
    @pl.when(jnp.logical_and(pl.program_id(0) == 0, pl.program_id(1) == 0))
    def _():
        row = lax.broadcasted_iota(jnp.int32, (CHUNK, CHUNK), 0)
        col = lax.broadcasted_iota(jnp.int32, (CHUNK, CHUNK), 1)
        rel = row - col
        rel_f = jnp.maximum(rel, 0).astype(F32)
        row_f = row.astype(F32)
        for hd in range(RET_HEADS):
            lg = RET_LOG_GAMMA[hd]
            decay_ref[hd] = jnp.where(rel >= 0, jnp.exp(lg * rel_f), 0.0)
            zeta_ref[hd] = jnp.exp(lg * (CHUNK - 1.0 - row_f))
            xi_ref[hd] = jnp.exp(lg * (row_f + 1.0))

    @pl.when(pl.program_id(1) == 0)
    def _():
        state_ref[...] = init_ref[...]

    cos = cos_ref[...]
    sin_signed = sin_ref[...]
    cos_k = cosk_ref[...]
    sin_k = sink_ref[...]

    heads = range(RET_HEADS)
    sls = [slice(hd * RET_DK, (hd + 1) * RET_DK) for hd in heads]
    scores, kvs, cross_lhs, out_rhs = [], [], [], []
    for hd in heads:
        q = q_ref[:, sls[hd]]
        q = q * cos + pltpu.roll(q, RET_DK // 2, 1) * sin_signed
        k = k_ref[:, sls[hd]]
        k = k * cos_k + pltpu.roll(k, RET_DK // 2, 1) * sin_k
        vb = v_ref[:, sls[hd]].astype(BF16)
        scores.append(lax.dot_general(q.astype(BF16), k.astype(BF16), (((1,), (1,)), ((), ())),
                                      preferred_element_type=F32))
        kvs.append(lax.dot_general((k * zeta_ref[hd]).astype(BF16), vb, (((0,), (0,)), ((), ())),
                                   preferred_element_type=F32))
        cross_lhs.append((q * xi_ref[hd]).astype(BF16))
        out_rhs.append(jnp.concatenate([vb, state_ref[hd].astype(BF16)], axis=0))
    outs = []
    for hd in heads:
        lhs = jnp.concatenate([(scores[hd] * decay_ref[hd]).astype(BF16), cross_lhs[hd]], axis=1)
        outs.append(jnp.dot(lhs, out_rhs[hd], preferred_element_type=F32))
        state_ref[hd] = math.exp(RET_LOG_GAMMA[hd] * CHUNK) * state_ref[hd] + kvs[hd]
    for hd in heads:
        gate = g_ref[:, sls[hd]]
        y = _head_norm(outs[hd], gn_ref[:, sls[hd]]) * (gate * jax.nn.sigmoid(gate))
        o_ref[:, sls[hd]] = y.astype(BF16)


def _mlstm_kernel(qk_ref, v_ref, og_ref, gates_ref, gbias_ref, cw_ref, cb_ref, gn_ref,
                  c0_ref, n0_ref, m0_ref, tail0_ref,
                  o_ref, c_ref, n_ref, m_ref, tail_ref, xe_ref, *, mask_pad):
    hist = CONV_HIST

    nblk = MLSTM_QK // LANES
    lane_blk = [slice(c * LANES, (c + 1) * LANES) for c in range(nblk)]

    @pl.when(pl.program_id(1) == 0)
    def _():
        for c in range(nblk):
            xe_ref[c, 0:hist, :] = tail0_ref[:, lane_blk[c]]
        c_ref[...] = c0_ref[...]
        n_ref[...] = n0_ref[...]
        m_ref[...] = m0_ref[...]

    cw = cw_ref[...]
    cb = cb_ref[...]
    qk = []
    for c in range(nblk):
        xe_ref[c, hist:hist + CHUNK, :] = qk_ref[:, lane_blk[c]]
        y = cb[:, lane_blk[c]]
        for tap in range(MLSTM_CONV):
            off = hist - (MLSTM_CONV - 1) + tap
            y = y + cw[tap:tap + 1, lane_blk[c]] * xe_ref[c, off:off + CHUNK, :]
        tail = xe_ref[c, CHUNK:CHUNK + hist, :]
        xe_ref[c, 0:hist, :] = tail
        tail_ref[:, lane_blk[c]] = tail
        qk.append(y * jax.nn.sigmoid(y))

    lane = lax.broadcasted_iota(jnp.int32, (CHUNK, LANES), 1)
    rowi = lax.broadcasted_iota(jnp.int32, (CHUNK, LANES), 0)
    pre = gates_ref[...] + gbias_ref[...]
    log_f = jnp.minimum(pre, 0.0) - jnp.log1p(jnp.exp(-jnp.abs(pre)))
    is_f = jnp.logical_and(lane >= MLSTM_HEADS, lane < N_GATES)
    log_i = pre
    if mask_pad:
        valid = rowi >= PAD
        is_f = jnp.logical_and(is_f, valid)
        log_i = jnp.where(valid, pre, NEG_INF)
    log_f = jnp.where(is_f, log_f, 0.0)
    tri = (rowi >= lane).astype(F32)
    b_cols = jnp.dot(tri, log_f, precision=lax.Precision.HIGHEST, preferred_element_type=F32)
    b_rows = b_cols.T
    log_i_rows = log_i.T
    causal = rowi >= lane

    for hd in range(MLSTM_HEADS):
        q = qk[hd] * (MLSTM_DK ** -0.5)
        k = qk[MLSTM_HEADS + hd]
        vsl = slice(hd * MLSTM_DV, (hd + 1) * MLSTM_DV)
        vb = v_ref[:, vsl].astype(BF16)
        qb = q.astype(BF16)
        fcol = MLSTM_HEADS + hd
        b_col = b_cols[:, fcol:fcol + 1]
        b_row = b_rows[fcol:fcol + 1, :]
        li_col = log_i[:, hd:hd + 1]
        li_row = log_i_rows[hd:hd + 1, :]
        g_tot = b_cols[CHUNK - 1:CHUNK, fcol:fcol + 1]
        m_prev = m_ref[hd][0:1, 0:1]
        c_prev = c_ref[hd]
        n_prev = n_ref[hd][0:1, :]

        dmat = jnp.where(causal, b_col - b_row + li_row, NEG_INF)
        m_intra = jnp.max(dmat, axis=-1, keepdims=True)
        a_col = g_tot - b_col + li_col
        m_loc = jnp.max(a_col, axis=0, keepdims=True)
        kw = k * jnp.exp(a_col - m_loc)
        kv_loc = lax.dot_general(kw.astype(BF16), vb, (((0,), (0,)), ((), ())),
                                 preferred_element_type=F32)
        n_loc = jnp.sum(kw, axis=0, keepdims=True)

        m_t = jnp.maximum(b_col + m_prev, m_intra)
        s_inter = jnp.exp(b_col + m_prev - m_t)
        p = lax.dot_general(qb, k.astype(BF16), (((1,), (1,)), ((), ())),
                            preferred_element_type=F32) * jnp.exp(dmat - m_t)
        num = jnp.dot(p.astype(BF16), vb, preferred_element_type=F32) \
            + s_inter * jnp.dot(qb, c_prev.astype(BF16), preferred_element_type=F32)
        den = jnp.sum(p, axis=-1, keepdims=True) \
            + s_inter * jnp.sum(q * n_prev, axis=-1, keepdims=True)
        hout = num / jnp.maximum(jnp.abs(den), jnp.exp(-m_t))

        m_new = jnp.maximum(g_tot + m_prev, m_loc)
        s_old = jnp.exp(g_tot + m_prev - m_new)
        s_loc = jnp.exp(m_loc - m_new)
        c_ref[hd] = s_old * c_prev + s_loc * kv_loc
        n_ref[hd] = jnp.broadcast_to(s_old * n_prev + s_loc * n_loc, (SUBLANES, MLSTM_DK))
        m_ref[hd] = jnp.broadcast_to(m_new, (SUBLANES, LANES))

        og = og_ref[:, vsl]
        o_ref[:, vsl] = (_head_norm(hout, gn_ref[:, vsl]) * jax.nn.sigmoid(og)).astype(BF16)


def _mlstm_state_shapes():
    return [jax.ShapeDtypeStruct((MLSTM_HEADS, MLSTM_DK, MLSTM_DV), F32),
            jax.ShapeDtypeStruct((MLSTM_HEADS, SUBLANES, MLSTM_DK), F32),
            jax.ShapeDtypeStruct((MLSTM_HEADS, SUBLANES, LANES), F32),
            jax.ShapeDtypeStruct((CONV_HIST, MLSTM_QK), F32)]


N_RET_IN = 10
N_MLSTM_IN = 12


def _mixers_kernel(*refs, mask_pad):
    ret_in = refs[:N_RET_IN]
    mlstm_in = refs[N_RET_IN:N_RET_IN + N_MLSTM_IN]
    mix_ref, ret_state, c_ref, n_ref, m_ref, tail_ref, decay_ref, zeta_ref, xi_ref, xe_ref = \
        refs[N_RET_IN + N_MLSTM_IN:]
    _retention_kernel(*ret_in, mix_ref.at[:, 0:RET_WIDTH], ret_state, decay_ref, zeta_ref, xi_ref)
    _mlstm_kernel(*mlstm_in, mix_ref.at[:, RET_WIDTH:RET_WIDTH + MLSTM_WIDTH], c_ref, n_ref, m_ref, tail_ref,
                  xe_ref, mask_pad=mask_pad)


def _mixers(proj, gates, rope, ret_init, mlstm_init, nseq, nchunk, mask_pad, *,
            ret_gain, gate_bias, conv_w, conv_b, mlstm_gain):
    m = proj.shape[0]
    qk_blk = 4 * RET_WIDTH // MLSTM_QK
    v_blk = (4 * RET_WIDTH + MLSTM_QK) // MLSTM_WIDTH
    rows = lambda b, n: b * nchunk + n
    const = lambda shape: pl.BlockSpec(shape, lambda b, n: (0,) * len(shape))
    ret_blk = lambda c: pl.BlockSpec((CHUNK, RET_WIDTH), lambda b, n: (rows(b, n), c))
    tab = pl.BlockSpec((CHUNK, RET_DK), lambda b, n: (n, 0))
    state_shapes = [jax.ShapeDtypeStruct((RET_HEADS, RET_DK, RET_DV), F32)] + _mlstm_state_shapes()
    state_specs = [const(s.shape) for s in state_shapes]
    ret_specs = [ret_blk(0), ret_blk(1), ret_blk(2), ret_blk(3), tab, tab, tab, tab,
                 const((1, RET_WIDTH)), state_specs[0]]
    mlstm_specs = [
        pl.BlockSpec((CHUNK, MLSTM_QK), lambda b, n: (rows(b, n), qk_blk)),
        pl.BlockSpec((CHUNK, MLSTM_WIDTH), lambda b, n: (rows(b, n), v_blk)),
        pl.BlockSpec((CHUNK, MLSTM_WIDTH), lambda b, n: (rows(b, n), v_blk + 1)),
        pl.BlockSpec((CHUNK, LANES), lambda b, n: (rows(b, n), 0)),
        const((1, LANES)),
        const((MLSTM_CONV, MLSTM_QK)),
        const((1, MLSTM_QK)),
        const((1, MLSTM_WIDTH)),
    ] + state_specs[1:]
    assert len(ret_specs) == N_RET_IN and len(mlstm_specs) == N_MLSTM_IN
    return pl.pallas_call(
        functools.partial(_mixers_kernel, mask_pad=mask_pad),
        grid=(nseq, nchunk),
        in_specs=ret_specs + mlstm_specs,
        out_specs=[pl.BlockSpec((CHUNK, D_MIX), lambda b, n: (rows(b, n), 0))] + state_specs,
        out_shape=[jax.ShapeDtypeStruct((m, D_MIX), BF16)] + state_shapes,
        scratch_shapes=[pltpu.VMEM((RET_HEADS, CHUNK, CHUNK), F32)] * 3
        + [pltpu.VMEM((MLSTM_QK // LANES, CHUNK + 2 * SUBLANES, LANES), F32)],
        compiler_params=_params("arbitrary", "arbitrary"),
        name="mixers",
    )(proj, proj, proj, proj, *rope, ret_gain, ret_init,
      proj, proj, proj, gates, gate_bias, conv_w, conv_b, mlstm_gain, *mlstm_init)


def _outproj_kernel(mix_ref, h_ref, w_ref, o_ref, *, is_meta):
    acc = h_ref[...] + jnp.dot(mix_ref[...], w_ref[...], preferred_element_type=F32)
    o_ref[...] = _zero_pad_rows(acc) if is_meta else acc


def _outproj(mix, h, w_out_bf16, layer, is_meta):
    m = h.shape[0]
    tm = min(TM_OUT, m)
    assert not is_meta or m == CHUNK
    return pl.pallas_call(
        functools.partial(_outproj_kernel, is_meta=is_meta),
        grid=(m // tm,),
        in_specs=[
            pl.BlockSpec((tm, D_MIX), lambda i: (i, 0)),
            pl.BlockSpec((tm, D_MODEL), lambda i: (i, 0)),
            pl.BlockSpec((None, D_MIX, D_MODEL), lambda i: (layer, 0, 0)),
        ],
        out_specs=pl.BlockSpec((tm, D_MODEL), lambda i: (i, 0)),
        out_shape=jax.ShapeDtypeStruct((m, D_MODEL), F32),
        compiler_params=_params("parallel"),
        name="outproj",
    )(mix, h, w_out_bf16)


def _ffn_kernel(h_ref, halo_ref, meta_halo_ref, g_ref, wg_ref, wv_ref, cp_ref,
                wd_ref, fg_ref, o_ref, u_ref, upg_ref, upv_ref, *, tiles_per_seq, is_meta, final_norm):
    f = pl.program_id(1)
    halo = BF16_ROWS
    tm = h_ref.shape[0]

    @pl.when(f == 0)
    def _():
        gain = g_ref[...]
        starts_seq = jnp.logical_and(not is_meta, pl.program_id(0) % tiles_per_seq == 0)

        @pl.when(starts_seq)
        def _():
            _rmsnorm_rows(meta_halo_ref, gain, u_ref, 0, halo)

        @pl.when(jnp.logical_not(starts_seq))
        def _():
            _rmsnorm_rows(halo_ref, gain, u_ref, 0, halo)

        _rmsnorm_rows(h_ref, gain, u_ref, halo, tm)
        o_ref[...] = h_ref[...]

    u = u_ref[...]

    nblk = TF_FFN // LANES

    def up_proj(w_ref, up_ref):
        up = jnp.dot(u, w_ref[...], preferred_element_type=F32)
        for c in range(nblk):
            up_ref[c] = up[:, c * LANES:(c + 1) * LANES]

    up_proj(wg_ref, upg_ref)
    up_proj(wv_ref, upv_ref)

    cp = cp_ref[...]

    def conv(up_ref, base):
        cols = []
        for c in range(nblk):
            sl = slice(c * LANES, (c + 1) * LANES)
            y = cp[base + FFN_CONV:base + FFN_CONV + 1, sl]
            for tap in range(FFN_CONV):
                off = halo - (FFN_CONV - 1) + tap
                y = y + cp[base + tap:base + tap + 1, sl] * up_ref[c, off:off + tm, :]
            cols.append(y)
        return jnp.concatenate(cols, axis=1)

    gate = conv(upg_ref, 0)
    val = conv(upv_ref, FFN_CONV + 1)
    act = (gate * jax.nn.sigmoid(gate) * val).astype(BF16)
    o_ref[...] += jnp.dot(act, wd_ref[...], preferred_element_type=F32)

    if is_meta or final_norm:
        @pl.when(f == pl.num_programs(1) - 1)
        def _():
            if is_meta:
                o_ref[...] = _zero_pad_rows(o_ref[...])
            if final_norm:
                _rmsnorm_rows(o_ref, fg_ref[...], o_ref, 0, tm)


def _ffn(h, meta_h, gain, w_up, conv_pack, w_down, layer, final_gain, seq, is_meta, final_norm):
    m = h.shape[0]
    tm = min(TM_FFN, m)
    nf = FFN_DIM // TF_FFN
    halo_blocks = tm // BF16_ROWS
    assert not is_meta or m == CHUNK
    return pl.pallas_call(
        functools.partial(_ffn_kernel, tiles_per_seq=seq // tm, is_meta=is_meta, final_norm=final_norm),
        grid=(m // tm, nf),
        in_specs=[
            pl.BlockSpec((tm, D_MODEL), lambda i, f: (i, 0)),
            pl.BlockSpec((BF16_ROWS, D_MODEL), lambda i, f: (jnp.maximum(i * halo_blocks - 1, 0), 0)),
            pl.BlockSpec((BF16_ROWS, D_MODEL), lambda i, f: (CHUNK // BF16_ROWS - 1, 0)),
            pl.BlockSpec((1, D_MODEL), lambda i, f: (0, 0)),
            pl.BlockSpec((None, D_MODEL, TF_FFN), lambda i, f: (layer, 0, f)),
            pl.BlockSpec((None, D_MODEL, TF_FFN), lambda i, f: (layer, 0, nf + f)),
            pl.BlockSpec((None, None, 2 * (FFN_CONV + 1), TF_FFN), lambda i, f: (layer, f, 0, 0)),
            pl.BlockSpec((None, TF_FFN, D_MODEL), lambda i, f: (layer, f, 0)),
            pl.BlockSpec((1, D_MODEL), lambda i, f: (0, 0)),
        ],
        out_specs=pl.BlockSpec((tm, D_MODEL), lambda i, f: (i, 0)),
        out_shape=jax.ShapeDtypeStruct((m, D_MODEL), F32),
        scratch_shapes=[
            pltpu.VMEM((tm + BF16_ROWS, D_MODEL), BF16),
            pltpu.VMEM((TF_FFN // LANES, tm + BF16_ROWS, LANES), F32),
            pltpu.VMEM((TF_FFN // LANES, tm + BF16_ROWS, LANES), F32),
        ],
        compiler_params=_params("parallel", "arbitrary"),
        name="convffn",
    )(h, h, meta_h, gain, w_up, w_up, conv_pack, w_down, final_gain)


def _pack_ffn_conv(conv_w, conv_b):
    depth = conv_w.shape[0]
    nf = FFN_DIM // TF_FFN
    w = conv_w.reshape(depth, FFN_CONV, 2, nf, TF_FFN)
    b = conv_b.reshape(depth, 1, 2, nf, TF_FFN)
    rows = jnp.concatenate([w[:, :, 0], b[:, :, 0], w[:, :, 1], b[:, :, 1]], axis=1)
    return jnp.swapaxes(rows, 1, 2)


def _rope_tables(pos):
    inv = ROPE_THETA ** (-jnp.arange(0, RET_DK, 2, dtype=F32) / RET_DK)
    ang = pos[:, None] * inv[None, :]
    cos = jnp.concatenate([jnp.cos(ang)] * 2, axis=-1)
    sin_signed = jnp.concatenate([-jnp.sin(ang), jnp.sin(ang)], axis=-1)
    return cos, sin_signed, cos * RET_DK ** -0.5, sin_signed * RET_DK ** -0.5


def kernel(x, meta_tokens, norm_mix, w_in, mlstm_conv_w, mlstm_conv_b, mlstm_b_i, mlstm_b_f, ret_norm, mlstm_norm, w_out, norm_ffn, w_up, ffn_conv_w, ffn_conv_b, w_down, norm_final):
    batch, seq, d = x.shape
    depth = w_in.shape[0]
    assert d == D_MODEL and w_in.shape[2] == D_MAIN + N_GATES and meta_tokens.shape[0] == N_META
    assert seq % TM_FFN == 0 and (batch * seq) % TM_PROJ == 0

    h = x.reshape(batch * seq, d)
    h_meta = jnp.concatenate([jnp.zeros((PAD, d), x.dtype), meta_tokens.astype(x.dtype)], axis=0)
    rope_meta = _rope_tables(jnp.maximum(jnp.arange(CHUNK, dtype=F32) - PAD, 0.0))
    rope_seq = _rope_tables(jnp.arange(N_META, N_META + seq, dtype=F32))
    ret_zero = jnp.zeros((RET_HEADS, RET_DK, RET_DV), F32)
    mlstm_zero = [jnp.zeros(s.shape, s.dtype) for s in _mlstm_state_shapes()]

    w_in_b, w_out_b, w_up_b, w_down_b = (w.astype(BF16) for w in (w_in, w_out, w_up, w_down))
    conv_pack = _pack_ffn_conv(ffn_conv_w, ffn_conv_b)
    final_gain = norm_final[None, :]

    for l in range(depth):
        last = l == depth - 1
        w_gate = jnp.pad(w_in[l, :, D_MAIN:], ((0, 0), (0, LANES - N_GATES))).astype(BF16)
        gate_bias = jnp.pad(jnp.concatenate([mlstm_b_i[l], mlstm_b_f[l]]), (0, LANES - N_GATES))[None, :]
        mix_gain = norm_mix[l][None, :]
        mixers = functools.partial(_mixers, ret_gain=ret_norm[l][None, :], gate_bias=gate_bias,
                                   conv_w=mlstm_conv_w[l], conv_b=mlstm_conv_b[l][None, :],
                                   mlstm_gain=mlstm_norm[l][None, :])
        ffn_args = (norm_ffn[l][None, :], w_up_b, conv_pack, w_down_b, l, final_gain, seq)

        proj, gates = _inproj(h_meta, mix_gain, w_in_b, l, w_gate)
        mix, ret_state, *mlstm_state = mixers(proj, gates, rope_meta, ret_zero, mlstm_zero, 1, 1, True)
        h_meta = _outproj(mix, h_meta, w_out_b, l, True)
        meta_ffn_in = h_meta
        if not last:
            h_meta = _ffn(h_meta, meta_ffn_in, *ffn_args, True, False)

        proj, gates = _inproj(h, mix_gain, w_in_b, l, w_gate)
        mix, *_ = mixers(proj, gates, rope_seq, ret_state, mlstm_state, batch, seq // CHUNK, False)
        h = _outproj(mix, h, w_out_b, l, False)
        h = _ffn(h, meta_ffn_in, *ffn_args, False, last)

    return h.reshape(batch, seq, d)
```

```python
import functools
import math

import jax
import jax.numpy as jnp
from jax import lax
from jax.experimental import pallas as pl
from jax.experimental.pallas import tpu as pltpu

F32 = jnp.float32
BF16 = jnp.bfloat16

D_MODEL = 2048
N_META = 16
CHUNK = 128
PAD = CHUNK - N_META
RET_HEADS = 8
RET_DK = 128
RET_DV = 128
RET_WIDTH = RET_HEADS * RET_DV
MLSTM_HEADS = 4
MLSTM_DK = 128
MLSTM_DV = 256
MLSTM_WIDTH = MLSTM_HEADS * MLSTM_DV
D_MIX = RET_WIDTH + MLSTM_WIDTH
MLSTM_QK = 2 * MLSTM_HEADS * MLSTM_DK
MLSTM_CONV = 4
FFN_DIM = 5632
FFN_CONV = 3
ROPE_THETA = 10000.0
EPS = 1e-6
D_MAIN = 4 * RET_WIDTH + MLSTM_QK + 2 * MLSTM_WIDTH
N_GATES = 2 * MLSTM_HEADS

LANES = 128
SUBLANES = 8
BF16_ROWS = 16
VMEM_LIMIT = 60 * 1024 * 1024

TM_PROJ = 1024
TN_PROJ = 1792
TM_OUT = 512
TM_FFN = 1024
TF_FFN = 512
RMS_ROWS = 256
CONV_HIST = SUBLANES

RET_LOG_GAMMA = [math.log1p(-(2.0 ** (-5.0 - h))) for h in range(RET_HEADS)]
NEG_INF = float("-inf")


def _params(*sem):
    return pltpu.CompilerParams(dimension_semantics=sem, vmem_limit_bytes=VMEM_LIMIT)


def _rmsnorm_rows(src_ref, gain, dst_ref, dst_off, nrows):
    blk = min(RMS_ROWS, nrows)

    def body(r, carry):
        r0 = pl.multiple_of(r * blk, blk)
        x = src_ref[pl.ds(r0, blk), :]
        ms = jnp.mean(x * x, axis=-1, keepdims=True)
        dst_ref[pl.ds(dst_off + r0, blk), :] = (x * lax.rsqrt(ms + EPS) * gain).astype(dst_ref.dtype)
        return carry

    lax.fori_loop(0, nrows // blk, body, 0)


def _zero_pad_rows(x):
    row = lax.broadcasted_iota(jnp.int32, (CHUNK, 1), 0)
    return jnp.where(row >= PAD, x, 0.0)


def _inproj_kernel(h_ref, g_ref, w_ref, wg_ref, proj_ref, gates_ref, u_ref):
    @pl.when(pl.program_id(1) == 0)
    def _():
        _rmsnorm_rows(h_ref, g_ref[...], u_ref, 0, h_ref.shape[0])
        gates_ref[...] = jnp.dot(u_ref[...], wg_ref[...], preferred_element_type=F32)

    proj_ref[...] = jnp.dot(u_ref[...], w_ref[...], preferred_element_type=F32)


def _inproj(h, gain, w_in_bf16, layer, w_gate):
    m = h.shape[0]
    tm = min(TM_PROJ, m)
    return pl.pallas_call(
        _inproj_kernel,
        grid=(m // tm, D_MAIN // TN_PROJ),
        in_specs=[
            pl.BlockSpec((tm, D_MODEL), lambda i, j: (i, 0)),
            pl.BlockSpec((1, D_MODEL), lambda i, j: (0, 0)),
            pl.BlockSpec((None, D_MODEL, TN_PROJ), lambda i, j: (layer, 0, j)),
            pl.BlockSpec((D_MODEL, LANES), lambda i, j: (0, 0)),
        ],
        out_specs=[
            pl.BlockSpec((tm, TN_PROJ), lambda i, j: (i, j)),
            pl.BlockSpec((tm, LANES), lambda i, j: (i, 0)),
        ],
        out_shape=[
            jax.ShapeDtypeStruct((m, D_MAIN), F32),
            jax.ShapeDtypeStruct((m, LANES), F32),
        ],
        scratch_shapes=[pltpu.VMEM((tm, D_MODEL), BF16)],
        compiler_params=_params("parallel", "arbitrary"),
        name="inproj",
    )(h, gain, w_in_bf16, w_gate)


def _head_norm(x, gain):
    mu = jnp.mean(x, axis=-1, keepdims=True)
    xc = x - mu
    return xc * lax.rsqrt(jnp.mean(xc * xc, axis=-1, keepdims=True) + EPS) * gain


def _retention_kernel(q_ref, k_ref, v_ref, g_ref, cos_ref, sin_ref, cosk_ref, sink_ref, gn_ref, init_ref,
                      o_ref, state_ref, decay_ref, zeta_ref, xi_ref):
    @pl.when(jnp.logical_and(pl.program_id(0) == 0, pl.program_id(1) == 0))
    def _():
        row = lax.broadcasted_iota(jnp.int32, (CHUNK, CHUNK), 0)
        col = lax.broadcasted_iota(jnp.int32, (CHUNK, CHUNK), 1)
        rel = row - col
        rel_f = jnp.maximum(rel, 0).astype(F32)
        row_f = row.astype(F32)
        for hd in range(RET_HEADS):
            lg = RET_LOG_GAMMA[hd]
            decay_ref[hd] = jnp.where(rel >= 0, jnp.exp(lg * rel_f), 0.0)
            zeta_ref[hd] = jnp.exp(lg * (CHUNK - 1.0 - row_f))
            xi_ref[hd] = jnp.exp(lg * (row_f + 1.0))

    @pl.when(pl.program_id(1) == 0)
    def _():
        state_ref[...] = init_ref[...]

    cos = cos_ref[...]
    sin_signed = sin_ref[...]
    cos_k = cosk_ref[...]
    sin_k = sink_ref[...]

    heads = range(RET_HEADS)
    sls = [slice(hd * RET_DK, (hd + 1) * RET_DK) for hd in heads]
    scores, kvs, cross_lhs, out_rhs = [], [], [], []
    for hd in heads:
        q = q_ref[:, sls[hd]]
        q = q * cos + pltpu.roll(q, RET_DK // 2, 1) * sin_signed
        k = k_ref[:, sls[hd]]
        k = k * cos_k + pltpu.roll(k, RET_DK // 2, 1) * sin_k
        vb = v_ref[:, sls[hd]].astype(BF16)
        scores.append(lax.dot_general(q.astype(BF16), k.astype(BF16), (((1,), (1,)), ((), ())),
                                      preferred_element_type=F32))
        kvs.append(lax.dot_general((k * zeta_ref[hd]).astype(BF16), vb, (((0,), (0,)), ((), ())),
                                   preferred_element_type=F32))
        cross_lhs.append((q * xi_ref[hd]).astype(BF16))
        out_rhs.append(jnp.concatenate([vb, state_ref[hd].astype(BF16)], axis=0))
    outs = []
    for hd in heads:
        lhs = jnp.concatenate([(scores[hd] * decay_ref[hd]).astype(BF16), cross_lhs[hd]], axis=1)
        outs.append(jnp.dot(lhs, out_rhs[hd], preferred_element_type=F32))
        state_ref[hd] = math.exp(RET_LOG_GAMMA[hd] * CHUNK) * state_ref[hd] + kvs[hd]
    for hd in heads:
        gate = g_ref[:, sls[hd]]
        y = _head_norm(outs[hd], gn_ref[:, sls[hd]]) * (gate * jax.nn.sigmoid(gate))
        o_ref[:, sls[hd]] = y.astype(BF16)


def _mlstm_kernel(qk_ref, v_ref, og_ref, gates_ref, gbias_ref, cw_ref, cb_ref, gn_ref,
                  c0_ref, n0_ref, m0_ref, tail0_ref,
                  o_ref, c_ref, n_ref, m_ref, tail_ref, xe_ref, *, mask_pad):
    hist = CONV_HIST

    nblk = MLSTM_QK // LANES
    lane_blk = [slice(c * LANES, (c + 1) * LANES) for c in range(nblk)]

    @pl.when(pl.program_id(1) == 0)
    def _():
        for c in range(nblk):
            xe_ref[c, 0:hist, :] = tail0_ref[:, lane_blk[c]]
        c_ref[...] = c0_ref[...]
        n_ref[...] = n0_ref[...]
        m_ref[...] = m0_ref[...]

    cw = cw_ref[...]
    cb = cb_ref[...]
    qk = []
    for c in range(nblk):
        xe_ref[c, hist:hist + CHUNK, :] = qk_ref[:, lane_blk[c]]
        y = cb[:, lane_blk[c]]
        for tap in range(MLSTM_CONV):
            off = hist - (MLSTM_CONV - 1) + tap
            y = y + cw[tap:tap + 1, lane_blk[c]] * xe_ref[c, off:off + CHUNK, :]
        tail = xe_ref[c, CHUNK:CHUNK + hist, :]
        xe_ref[c, 0:hist, :] = tail
        tail_ref[:, lane_blk[c]] = tail
        qk.append(y * jax.nn.sigmoid(y))

    lane = lax.broadcasted_iota(jnp.int32, (CHUNK, LANES), 1)
    rowi = lax.broadcasted_iota(jnp.int32, (CHUNK, LANES), 0)
    pre = gates_ref[...] + gbias_ref[...]
    log_f = jnp.minimum(pre, 0.0) - jnp.log1p(jnp.exp(-jnp.abs(pre)))
    is_f = jnp.logical_and(lane >= MLSTM_HEADS, lane < N_GATES)
    log_i = pre
    if mask_pad:
        valid = rowi >= PAD
        is_f = jnp.logical_and(is_f, valid)
        log_i = jnp.where(valid, pre, NEG_INF)
    log_f = jnp.where(is_f, log_f, 0.0)
    tri = (rowi >= lane).astype(F32)
    b_cols = jnp.dot(tri, log_f, precision=lax.Precision.HIGHEST, preferred_element_type=F32)
    b_rows = b_cols.T
    log_i_rows = log_i.T
    causal = rowi >= lane

    for hd in range(MLSTM_HEADS):
        q = qk[hd] * (MLSTM_DK ** -0.5)
        k = qk[MLSTM_HEADS + hd]
        vsl = slice(hd * MLSTM_DV, (hd + 1) * MLSTM_DV)
        vb = v_ref[:, vsl].astype(BF16)
        qb = q.astype(BF16)
        fcol = MLSTM_HEADS + hd
        b_col = b_cols[:, fcol:fcol + 1]
        b_row = b_rows[fcol:fcol + 1, :]
        li_col = log_i[:, hd:hd + 1]
        li_row = log_i_rows[hd:hd + 1, :]
        g_tot = b_cols[CHUNK - 1:CHUNK, fcol:fcol + 1]
        m_prev = m_ref[hd][0:1, 0:1]
        c_prev = c_ref[hd]
        n_prev = n_ref[hd][0:1, :]

        dmat = jnp.where(causal, b_col - b_row + li_row, NEG_INF)
        m_intra = jnp.max(dmat, axis=-1, keepdims=True)
        a_col = g_tot - b_col + li_col
        m_loc = jnp.max(a_col, axis=0, keepdims=True)
        kw = k * jnp.exp(a_col - m_loc)
        kv_loc = lax.dot_general(kw.astype(BF16), vb, (((0,), (0,)), ((), ())),
                                 preferred_element_type=F32)
        n_loc = jnp.sum(kw, axis=0, keepdims=True)

        m_t = jnp.maximum(b_col + m_prev, m_intra)
        s_inter = jnp.exp(b_col + m_prev - m_t)
        p = lax.dot_general(qb, k.astype(BF16), (((1,), (1,)), ((), ())),
                            preferred_element_type=F32) * jnp.exp(dmat - m_t)
        num = jnp.dot(p.astype(BF16), vb, preferred_element_type=F32) \
            + s_inter * jnp.dot(qb, c_prev.astype(BF16), preferred_element_type=F32)
        den = jnp.sum(p, axis=-1, keepdims=True) \
            + s_inter * jnp.sum(q * n_prev, axis=-1, keepdims=True)
        hout = num / jnp.maximum(jnp.abs(den), jnp.exp(-m_t))

        m_new = jnp.maximum(g_tot + m_prev, m_loc)
        s_old = jnp.exp(g_tot + m_prev - m_new)
        s_loc = jnp.exp(m_loc - m_new)
        c_ref[hd] = s_old * c_prev + s_loc * kv_loc
        n_ref[hd] = jnp.broadcast_to(s_old * n_prev + s_loc * n_loc, (SUBLANES, MLSTM_DK))
        m_ref[hd] = jnp.broadcast_to(m_new, (SUBLANES, LANES))

        og = og_ref[:, vsl]
        o_ref[:, vsl] = (_head_norm(hout, gn_ref[:, vsl]) * jax.nn.sigmoid(og)).astype(BF16)


def _mlstm_state_shapes():
    return [jax.ShapeDtypeStruct((MLSTM_HEADS, MLSTM_DK, MLSTM_DV), F32),
            jax.ShapeDtypeStruct((MLSTM_HEADS, SUBLANES, MLSTM_DK), F32),
            jax.ShapeDtypeStruct((MLSTM_HEADS, SUBLANES, LANES), F32),
            jax.ShapeDtypeStruct((CONV_HIST, MLSTM_QK), F32)]


def _mixers_kernel(proj_ref, rope_ref, gates_ref, ret_gain_ref, ret_init_ref, gbias_ref, cw_ref, cb_ref,
                   mlstm_gain_ref, c0_ref, n0_ref, m0_ref, tail0_ref,
                   mix_ref, ret_state, c_ref, n_ref, m_ref, tail_ref,
                   decay_ref, zeta_ref, xi_ref, xe_ref, *, mask_pad):
    def cols(ref, start, width):
        return ref.at[:, start:start + width]

    q, k, v, g = (cols(proj_ref, i * RET_WIDTH, RET_WIDTH) for i in range(4))
    qk = cols(proj_ref, 4 * RET_WIDTH, MLSTM_QK)
    vm = cols(proj_ref, 4 * RET_WIDTH + MLSTM_QK, MLSTM_WIDTH)
    og = cols(proj_ref, 4 * RET_WIDTH + MLSTM_QK + MLSTM_WIDTH, MLSTM_WIDTH)
    tables = [cols(rope_ref, i * RET_DK, RET_DK) for i in range(4)]
    _retention_kernel(q, k, v, g, *tables, ret_gain_ref, ret_init_ref,
                      cols(mix_ref, 0, RET_WIDTH), ret_state, decay_ref, zeta_ref, xi_ref)
    _mlstm_kernel(qk, vm, og, gates_ref, gbias_ref, cw_ref, cb_ref, mlstm_gain_ref,
                  c0_ref, n0_ref, m0_ref, tail0_ref,
                  cols(mix_ref, RET_WIDTH, MLSTM_WIDTH), c_ref, n_ref, m_ref, tail_ref,
                  xe_ref, mask_pad=mask_pad)


def _mixers(proj, gates, rope, ret_init, mlstm_init, nseq, nchunk, mask_pad, *,
            ret_gain, gate_bias, conv_w, conv_b, mlstm_gain):
    m = proj.shape[0]
    rows = lambda b, n: b * nchunk + n
    const = lambda shape: pl.BlockSpec(shape, lambda b, n: (0,) * len(shape))
    state_shapes = [jax.ShapeDtypeStruct((RET_HEADS, RET_DK, RET_DV), F32)] + _mlstm_state_shapes()
    state_specs = [const(s.shape) for s in state_shapes]
    in_specs = [
        pl.BlockSpec((CHUNK, D_MAIN), lambda b, n: (rows(b, n), 0)),
        pl.BlockSpec((CHUNK, 4 * RET_DK), lambda b, n: (n, 0)),
        pl.BlockSpec((CHUNK, LANES), lambda b, n: (rows(b, n), 0)),
        const((1, RET_WIDTH)),
        state_specs[0],
        const((1, LANES)),
        const((MLSTM_CONV, MLSTM_QK)),
        const((1, MLSTM_QK)),
        const((1, MLSTM_WIDTH)),
    ] + state_specs[1:]
    return pl.pallas_call(
        functools.partial(_mixers_kernel, mask_pad=mask_pad),
        grid=(nseq, nchunk),
        in_specs=in_specs,
        out_specs=[pl.BlockSpec((CHUNK, D_MIX), lambda b, n: (rows(b, n), 0))] + state_specs,
        out_shape=[jax.ShapeDtypeStruct((m, D_MIX), BF16)] + state_shapes,
        scratch_shapes=[pltpu.VMEM((RET_HEADS, CHUNK, CHUNK), F32)] * 3
        + [pltpu.VMEM((MLSTM_QK // LANES, CHUNK + 2 * SUBLANES, LANES), F32)],
        compiler_params=_params("arbitrary", "arbitrary"),
        name="mixers",
    )(proj, rope, gates, ret_gain, ret_init, gate_bias, conv_w, conv_b, mlstm_gain, *mlstm_init)


def _outproj_kernel(mix_ref, h_ref, w_ref, o_ref, *, is_meta):
    acc = h_ref[...] + jnp.dot(mix_ref[...], w_ref[...], preferred_element_type=F32)
    o_ref[...] = _zero_pad_rows(acc) if is_meta else acc


def _outproj(mix, h, w_out_bf16, layer, is_meta):
    m = h.shape[0]
    tm = min(TM_OUT, m)
    assert not is_meta or m == CHUNK
    return pl.pallas_call(
        functools.partial(_outproj_kernel, is_meta=is_meta),
        grid=(m // tm,),
        in_specs=[
            pl.BlockSpec((tm, D_MIX), lambda i: (i, 0)),
            pl.BlockSpec((tm, D_MODEL), lambda i: (i, 0)),
            pl.BlockSpec((None, D_MIX, D_MODEL), lambda i: (layer, 0, 0)),
        ],
        out_specs=pl.BlockSpec((tm, D_MODEL), lambda i: (i, 0)),
        out_shape=jax.ShapeDtypeStruct((m, D_MODEL), F32),
        compiler_params=_params("parallel"),
        name="outproj",
    )(mix, h, w_out_bf16)


def _ffn_kernel(h_ref, halo_ref, meta_halo_ref, g_ref, wg_ref, wv_ref, cp_ref,
                wd_ref, fg_ref, o_ref, u_ref, upg_ref, upv_ref, *, tiles_per_seq, is_meta, final_norm):
    f = pl.program_id(1)
    halo = BF16_ROWS
    tm = h_ref.shape[0]

    @pl.when(f == 0)
    def _():
        gain = g_ref[...]
        starts_seq = jnp.logical_and(not is_meta, pl.program_id(0) % tiles_per_seq == 0)

        @pl.when(starts_seq)
        def _():
            _rmsnorm_rows(meta_halo_ref, gain, u_ref, 0, halo)

        @pl.when(jnp.logical_not(starts_seq))
        def _():
            _rmsnorm_rows(halo_ref, gain, u_ref, 0, halo)

        _rmsnorm_rows(h_ref, gain, u_ref, halo, tm)
        o_ref[...] = h_ref[...]

    u = u_ref[...]

    nblk = TF_FFN // LANES

    def up_proj(w_ref, up_ref):
        up = jnp.dot(u, w_ref[...], preferred_element_type=F32)
        for c in range(nblk):
            up_ref[c] = up[:, c * LANES:(c + 1) * LANES]

    up_proj(wg_ref, upg_ref)
    up_proj(wv_ref, upv_ref)

    cp = cp_ref[...]

    def conv(up_ref, base):
        cols = []
        for c in range(nblk):
            sl = slice(c * LANES, (c + 1) * LANES)
            y = cp[base + FFN_CONV:base + FFN_CONV + 1, sl]
            for tap in range(FFN_CONV):
                off = halo - (FFN_CONV - 1) + tap
                y = y + cp[base + tap:base + tap + 1, sl] * up_ref[c, off:off + tm, :]
            cols.append(y)
        return jnp.concatenate(cols, axis=1)

    gate = conv(upg_ref, 0)
    val = conv(upv_ref, FFN_CONV + 1)
    act = (gate * jax.nn.sigmoid(gate) * val).astype(BF16)
    o_ref[...] += jnp.dot(act, wd_ref[...], preferred_element_type=F32)

    if is_meta or final_norm:
        @pl.when(f == pl.num_programs(1) - 1)
        def _():
            if is_meta:
                o_ref[...] = _zero_pad_rows(o_ref[...])
            if final_norm:
                _rmsnorm_rows(o_ref, fg_ref[...], o_ref, 0, tm)


def _ffn(h, meta_h, gain, w_up, conv_pack, w_down, layer, final_gain, seq, is_meta, final_norm):
    m = h.shape[0]
    tm = min(TM_FFN, m)
    nf = FFN_DIM // TF_FFN
    halo_blocks = tm // BF16_ROWS
    assert not is_meta or m == CHUNK
    return pl.pallas_call(
        functools.partial(_ffn_kernel, tiles_per_seq=seq // tm, is_meta=is_meta, final_norm=final_norm),
        grid=(m // tm, nf),
        in_specs=[
            pl.BlockSpec((tm, D_MODEL), lambda i, f: (i, 0)),
            pl.BlockSpec((BF16_ROWS, D_MODEL), lambda i, f: (jnp.maximum(i * halo_blocks - 1, 0), 0)),
            pl.BlockSpec((BF16_ROWS, D_MODEL), lambda i, f: (CHUNK // BF16_ROWS - 1, 0)),
            pl.BlockSpec((1, D_MODEL), lambda i, f: (0, 0)),
            pl.BlockSpec((None, D_MODEL, TF_FFN), lambda i, f: (layer, 0, f)),
            pl.BlockSpec((None, D_MODEL, TF_FFN), lambda i, f: (layer, 0, nf + f)),
            pl.BlockSpec((None, None, 2 * (FFN_CONV + 1), TF_FFN), lambda i, f: (layer, f, 0, 0)),
            pl.BlockSpec((None, TF_FFN, D_MODEL), lambda i, f: (layer, f, 0)),
            pl.BlockSpec((1, D_MODEL), lambda i, f: (0, 0)),
        ],
        out_specs=pl.BlockSpec((tm, D_MODEL), lambda i, f: (i, 0)),
        out_shape=jax.ShapeDtypeStruct((m, D_MODEL), F32),
        scratch_shapes=[
            pltpu.VMEM((tm + BF16_ROWS, D_MODEL), BF16),
            pltpu.VMEM((TF_FFN // LANES, tm + BF16_ROWS, LANES), F32),
            pltpu.VMEM((TF_FFN // LANES, tm + BF16_ROWS, LANES), F32),
        ],
        compiler_params=_params("parallel", "arbitrary"),
        name="convffn",
    )(h, h, meta_h, gain, w_up, w_up, conv_pack, w_down, final_gain)


def _pack_ffn_conv(conv_w, conv_b):
    depth = conv_w.shape[0]
    nf = FFN_DIM // TF_FFN
    w = conv_w.reshape(depth, FFN_CONV, 2, nf, TF_FFN)
    b = conv_b.reshape(depth, 1, 2, nf, TF_FFN)
    rows = jnp.concatenate([w[:, :, 0], b[:, :, 0], w[:, :, 1], b[:, :, 1]], axis=1)
    return jnp.swapaxes(rows, 1, 2)


def _rope_tables(pos):
    inv = ROPE_THETA ** (-jnp.arange(0, RET_DK, 2, dtype=F32) / RET_DK)
    ang = pos[:, None] * inv[None, :]
    cos = jnp.concatenate([jnp.cos(ang)] * 2, axis=-1)
    sin_signed = jnp.concatenate([-jnp.sin(ang), jnp.sin(ang)], axis=-1)
    return jnp.concatenate([cos, sin_signed, cos * RET_DK ** -0.5, sin_signed * RET_DK ** -0.5], axis=-1)


def kernel(x, meta_tokens, norm_mix, w_in, mlstm_conv_w, mlstm_conv_b, mlstm_b_i, mlstm_b_f, ret_norm, mlstm_norm, w_out, norm_ffn, w_up, ffn_conv_w, ffn_conv_b, w_down, norm_final):
    batch, seq, d = x.shape
    depth = w_in.shape[0]
    assert d == D_MODEL and w_in.shape[2] == D_MAIN + N_GATES and meta_tokens.shape[0] == N_META
    assert seq % TM_FFN == 0 and (batch * seq) % TM_PROJ == 0

    h = x.reshape(batch * seq, d)
    h_meta = jnp.concatenate([jnp.zeros((PAD, d), x.dtype), meta_tokens.astype(x.dtype)], axis=0)
    rope_meta = _rope_tables(jnp.maximum(jnp.arange(CHUNK, dtype=F32) - PAD, 0.0))
    rope_seq = _rope_tables(jnp.arange(N_META, N_META + seq, dtype=F32))
    ret_zero = jnp.zeros((RET_HEADS, RET_DK, RET_DV), F32)
    mlstm_zero = [jnp.zeros(s.shape, s.dtype) for s in _mlstm_state_shapes()]

    w_in_b, w_out_b, w_up_b, w_down_b = (w.astype(BF16) for w in (w_in, w_out, w_up, w_down))
    conv_pack = _pack_ffn_conv(ffn_conv_w, ffn_conv_b)
    final_gain = norm_final[None, :]

    for l in range(depth):
        last = l == depth - 1
        w_gate = jnp.pad(w_in[l, :, D_MAIN:], ((0, 0), (0, LANES - N_GATES))).astype(BF16)
        gate_bias = jnp.pad(jnp.concatenate([mlstm_b_i[l], mlstm_b_f[l]]), (0, LANES - N_GATES))[None, :]
        mix_gain = norm_mix[l][None, :]
        mixers = functools.partial(_mixers, ret_gain=ret_norm[l][None, :], gate_bias=gate_bias,
                                   conv_w=mlstm_conv_w[l], conv_b=mlstm_conv_b[l][None, :],
                                   mlstm_gain=mlstm_norm[l][None, :])
        ffn_args = (norm_ffn[l][None, :], w_up_b, conv_pack, w_down_b, l, final_gain, seq)

        proj, gates = _inproj(h_meta, mix_gain, w_in_b, l, w_gate)
        mix, ret_state, *mlstm_state = mixers(proj, gates, rope_meta, ret_zero, mlstm_zero, 1, 1, True)
        h_meta = _outproj(mix, h_meta, w_out_b, l, True)
        meta_ffn_in = h_meta
        if not last:
            h_meta = _ffn(h_meta, meta_ffn_in, *ffn_args, True, False)

        proj, gates = _inproj(h, mix_gain, w_in_b, l, w_gate)
        mix, *_ = mixers(proj, gates, rope_seq, ret_state, mlstm_state, batch, seq // CHUNK, False)
        h = _outproj(mix, h, w_out_b, l, False)
        h = _ffn(h, meta_ffn_in, *ffn_args, False, last)

    return h.reshape(batch, seq, d)
```
